```python
import math
import jax, jax.numpy as jnp
from jax import lax
import numpy as np

D_MODEL = 1024
BATCH = 32
SEQ = 2048
DEPTH = 4

N_MIXERS = 3
N_SUBLAYERS = 3
N_MOD = 3
Q_BLOCK = 128
LN_EPS = 1e-5
RMS_EPS = 1e-6
DEEPNORM_ALPHA = (2.0 * DEPTH) ** 0.25
DEEPNORM_BETA = (8.0 * DEPTH) ** -0.25
FFN_HIDDEN = ((8 * D_MODEL // 3 + 255) // 256) * 256
FFN_RES_WEIGHT = 0.5

MLA_HEADS = D_MODEL // 128
MLA_Q_RANK = 3 * D_MODEL // 8
MLA_KV_RANK = D_MODEL // 4
MLA_NOPE_DIM = 128
MLA_ROPE_DIM = 64
MLA_V_DIM = 128
ROPE_THETA = 10000.0

DIFF_HEAD_DIM = 64
DIFF_HEADS = D_MODEL // (2 * DIFF_HEAD_DIM)
LAMBDA_STD = 0.1

SSM_INNER = 2 * D_MODEL
SSM_HEAD_DIM = 64
SSM_HEADS = SSM_INNER // SSM_HEAD_DIM
SSM_STATE = 128
SSM_GROUPS = 4
SSM_CONV = 4
SSM_CHUNK = 128
SSM_CONV_DIM = SSM_INNER + 2 * SSM_GROUPS * SSM_STATE

N_MLA_LAYERS = (DEPTH + 2) // 3
N_DIFF_LAYERS = (DEPTH + 1) // 3
N_SSM_LAYERS = DEPTH // 3

kernel_name = "hybrid_mla_diff_ssd_macaron_deepnorm"


def _layernorm(x, g, b):
    xf = x.astype(jnp.float32)
    mu = jnp.mean(xf, axis=-1, keepdims=True)
    var = jnp.mean(jnp.square(xf - mu), axis=-1, keepdims=True)
    return ((xf - mu) * lax.rsqrt(var + LN_EPS) * g + b).astype(x.dtype)


def _rmsnorm(x, g):
    xf = x.astype(jnp.float32)
    return (xf * lax.rsqrt(jnp.mean(xf * xf, axis=-1, keepdims=True) + RMS_EPS) * g).astype(x.dtype)


def _swiglu(h, w_in, w_out):
    gate, up = jnp.split(h @ w_in, 2, axis=-1)
    return (jax.nn.silu(gate) * up) @ w_out


def _sublayer(x, fn, mod_j, g, b, weight):
    shift, scale, gate = mod_j[:, 0, None, :], mod_j[:, 1, None, :], mod_j[:, 2, None, :]
    y = fn(x * (1 + scale) + shift)
    return _layernorm(DEEPNORM_ALPHA * x + weight * gate * y, g, b)


def _causal_blocks(q, k, v, attend):
    seq = q.shape[1]
    outs = []
    for start in range(0, seq, Q_BLOCK):
        end = start + Q_BLOCK
        mask = jnp.arange(end)[None, :] <= jnp.arange(start, end)[:, None]
        outs.append(attend(q[:, start:end], k[:, :end], v[:, :end], mask))
    return jnp.concatenate(outs, axis=1)


def _rope(x, cos, sin):
    xf = x.astype(jnp.float32)
    x1, x2 = jnp.split(xf, 2, axis=-1)
    return jnp.concatenate([x1 * cos - x2 * sin, x1 * sin + x2 * cos], axis=-1).astype(x.dtype)


def _mla(h, positions, w_in, q_norm_g, kv_norm_g, w_q_up, w_kv_up, w_out):
    b, s, _ = h.shape
    cq, ckv, k_rope = jnp.split(h @ w_in, [MLA_Q_RANK, MLA_Q_RANK + MLA_KV_RANK], axis=-1)
    q = (_rmsnorm(cq, q_norm_g) @ w_q_up).reshape(b, s, MLA_HEADS, MLA_NOPE_DIM + MLA_ROPE_DIM)
    kv = (_rmsnorm(ckv, kv_norm_g) @ w_kv_up).reshape(b, s, MLA_HEADS, MLA_NOPE_DIM + MLA_V_DIM)
    q_nope, q_rope = q[..., :MLA_NOPE_DIM], q[..., MLA_NOPE_DIM:]
    k_nope, v = kv[..., :MLA_NOPE_DIM], kv[..., MLA_NOPE_DIM:]
    inv_freq = ROPE_THETA ** (-jnp.arange(0, MLA_ROPE_DIM, 2, dtype=jnp.float32) / MLA_ROPE_DIM)
    ang = positions.astype(jnp.float32)[..., None] * inv_freq
    cos, sin = jnp.cos(ang)[:, :, None, :], jnp.sin(ang)[:, :, None, :]
    q_rope = _rope(q_rope, cos, sin)
    k_rope = _rope(k_rope[:, :, None, :], cos, sin)
    q = jnp.concatenate([q_nope, q_rope], axis=-1)
    k = jnp.concatenate([k_nope, jnp.broadcast_to(k_rope, (b, s, MLA_HEADS, MLA_ROPE_DIM))], axis=-1)
    scale = (MLA_NOPE_DIM + MLA_ROPE_DIM) ** -0.5

    def attend(qb, kb, vb, mask):
        sc = jnp.einsum('bqhd,bkhd->bhqk', qb, kb).astype(jnp.float32) * scale
        p = jax.nn.softmax(jnp.where(mask, sc, -jnp.inf), axis=-1).astype(vb.dtype)
        return jnp.einsum('bhqk,bkhv->bqhv', p, vb)

    o = _causal_blocks(q, k, v, attend)
    return o.reshape(b, s, MLA_HEADS * MLA_V_DIM) @ w_out


def _diff_attn(h, layer, w_in, lambda_q, lambda_k, subln_g, w_out):
    b, s, _ = h.shape
    lam_init = 0.8 - 0.6 * math.exp(-0.3 * layer)
    q, k, v = jnp.split(h @ w_in, 3, axis=-1)
    q = q.reshape(b, s, DIFF_HEADS, 2, DIFF_HEAD_DIM)
    k = k.reshape(b, s, DIFF_HEADS, 2, DIFF_HEAD_DIM)
    v = v.reshape(b, s, DIFF_HEADS, 2 * DIFF_HEAD_DIM)
    lq, lk = lambda_q.astype(jnp.float32), lambda_k.astype(jnp.float32)
    lam = jnp.exp(jnp.sum(lq[0] * lk[0])) - jnp.exp(jnp.sum(lq[1] * lk[1])) + lam_init
    scale = DIFF_HEAD_DIM ** -0.5

    def attend(qb, kb, vb, mask):
        sc = jnp.einsum('bqhid,bkhid->bhiqk', qb, kb).astype(jnp.float32) * scale
        p = jax.nn.softmax(jnp.where(mask, sc, -jnp.inf), axis=-1)
        a = (p[:, :, 0] - lam * p[:, :, 1]).astype(vb.dtype)
        return jnp.einsum('bhqk,bkhe->bqhe', a, vb)

    o = _causal_blocks(q, k, v, attend)
    o = _rmsnorm(o, subln_g) * (1.0 - lam_init)
    return o.reshape(b, s, D_MODEL) @ w_out


def _ssd_scan(x, dt, a, bmat, cmat):
    b, s, nh, p = x.shape
    g, n = bmat.shape[2], bmat.shape[3]
    hg = nh // g
    nc, L = s // SSM_CHUNK, SSM_CHUNK

    def chunks(t):
        return jnp.moveaxis(t.reshape(b, nc, L, *t.shape[2:]), 1, 0)

    xs = chunks(x.reshape(b, s, g, hg, p))
    dts = chunks(dt.reshape(b, s, g, hg))
    bs, cs = chunks(bmat), chunks(cmat)
    causal = jnp.tril(jnp.ones((L, L), dtype=bool))[None, :, :, None, None]
    a_g = a.reshape(g, hg)

    def step(state, inp):
        xc, dtc, bc, cc = inp
        acum = jnp.cumsum(dtc * a_g, axis=1)
        seg = acum[:, :, None] - acum[:, None, :]
        decay = jnp.exp(jnp.where(causal, seg, -jnp.inf))
        cb = jnp.einsum('blgn,bsgn->blsg', cc, bc)
        y = jnp.einsum('blsg,blsgh,bsghp->blghp', cb, decay, xc * dtc[..., None])
        y = y + jnp.einsum('blgn,bghpn->blghp', cc, state) * jnp.exp(acum)[..., None]
        last = acum[:, -1]
        w = jnp.exp(last[:, None] - acum) * dtc
        state = state * jnp.exp(last)[..., None, None] + jnp.einsum('bsgn,bsgh,bsghp->bghpn', bc, w, xc)
        return state, y

    state0 = jnp.zeros((b, g, hg, p, n), jnp.float32)
    _, ys = lax.scan(step, state0, (xs, dts, bs, cs))
    return jnp.moveaxis(ys, 0, 1).reshape(b, s, nh, p)


def _mamba2(h, w_in, conv_w, conv_b, dt_bias, a_log, d_skip, norm_g, w_out):
    b, s, _ = h.shape
    z, xbc, dt = jnp.split(h @ w_in, [SSM_INNER, SSM_INNER + SSM_CONV_DIM], axis=-1)
    xbc = lax.conv_general_dilated(
        xbc, conv_w[:, None, :], window_strides=(1,), padding=((SSM_CONV - 1, 0),),
        dimension_numbers=('NWC', 'WIO', 'NWC'), feature_group_count=SSM_CONV_DIM) + conv_b
    xbc = jax.nn.silu(xbc)
    xs, bmat, cmat = jnp.split(xbc, [SSM_INNER, SSM_INNER + SSM_GROUPS * SSM_STATE], axis=-1)
    dt = jax.nn.softplus(dt.astype(jnp.float32) + dt_bias.astype(jnp.float32))
    a = -jnp.exp(a_log.astype(jnp.float32))
    xh = xs.reshape(b, s, SSM_HEADS, SSM_HEAD_DIM).astype(jnp.float32)
    y = _ssd_scan(xh, dt, a,
                  bmat.reshape(b, s, SSM_GROUPS, SSM_STATE).astype(jnp.float32),
                  cmat.reshape(b, s, SSM_GROUPS, SSM_STATE).astype(jnp.float32))
    y = y + d_skip.astype(jnp.float32)[:, None] * xh
    y = y.reshape(b, s, SSM_INNER).astype(h.dtype) * jax.nn.silu(z)
    gs = SSM_INNER // SSM_GROUPS
    y = _rmsnorm(y.reshape(b, s, SSM_GROUPS, gs), norm_g.reshape(SSM_GROUPS, gs)).reshape(b, s, SSM_INNER)
    return y @ w_out


def setup_inputs(seed: int = 0) -> dict:
    key = jax.random.key(seed)
    ks = iter(jax.random.split(key, 32))

    def nrm(shape, scale):
        return jax.random.normal(next(ks), shape, jnp.float32) * scale

    D, F = D_MODEL, FFN_HIDDEN
    x = nrm((BATCH, SEQ, D), 1.0)
    c = nrm((BATCH, D), 1.0)
    offsets = jax.random.randint(next(ks), (BATCH, 1), 0, 4096, dtype=jnp.int32)
    positions = offsets + jnp.arange(SEQ, dtype=jnp.int32)[None, :]
    w_mod = nrm((DEPTH, D, N_SUBLAYERS * N_MOD * D), 0.5 * D ** -0.5)
    b_mod = nrm((DEPTH, N_SUBLAYERS * N_MOD * D), 0.02)
    ln_g = 1.0 + nrm((DEPTH, N_SUBLAYERS, D), 0.02)
    ln_b = nrm((DEPTH, N_SUBLAYERS, D), 0.02)
    ffn_w_in = nrm((DEPTH, 2, D, 2 * F), D ** -0.5)
    ffn_w_out = nrm((DEPTH, 2, F, D), F ** -0.5 * DEEPNORM_BETA)

    na = N_MLA_LAYERS
    mla_w_in = nrm((na, D, MLA_Q_RANK + MLA_KV_RANK + MLA_ROPE_DIM), D ** -0.5)
    mla_q_norm_g = 1.0 + nrm((na, MLA_Q_RANK), 0.02)
    mla_kv_norm_g = 1.0 + nrm((na, MLA_KV_RANK), 0.02)
    mla_w_q_up = nrm((na, MLA_Q_RANK, MLA_HEADS * (MLA_NOPE_DIM + MLA_ROPE_DIM)), MLA_Q_RANK ** -0.5)
    mla_w_kv_up = nrm((na, MLA_KV_RANK, MLA_HEADS * (MLA_NOPE_DIM + MLA_V_DIM)), MLA_KV_RANK ** -0.5)
    mla_w_out = nrm((na, MLA_HEADS * MLA_V_DIM, D), (MLA_HEADS * MLA_V_DIM) ** -0.5 * DEEPNORM_BETA)

    nb = N_DIFF_LAYERS
    diff_w_in = nrm((nb, D, 3 * D), D ** -0.5)
    diff_lambda_q = nrm((nb, 2, DIFF_HEAD_DIM), LAMBDA_STD)
    diff_lambda_k = nrm((nb, 2, DIFF_HEAD_DIM), LAMBDA_STD)
    diff_subln_g = 1.0 + nrm((nb, 2 * DIFF_HEAD_DIM), 0.02)
    diff_w_out = nrm((nb, D, D), D ** -0.5 * DEEPNORM_BETA)

    nc = N_SSM_LAYERS
    ssm_w_in = nrm((nc, D, 2 * SSM_INNER + 2 * SSM_GROUPS * SSM_STATE + SSM_HEADS), D ** -0.5)
    ssm_conv_w = nrm((nc, SSM_CONV, SSM_CONV_DIM), SSM_CONV ** -0.5)
    ssm_conv_b = nrm((nc, SSM_CONV_DIM), 0.02)
    u = jax.random.uniform(next(ks), (nc, SSM_HEADS), jnp.float32)
    dt0 = jnp.exp(u * (math.log(0.1) - math.log(1e-3)) + math.log(1e-3))
    ssm_dt_bias = dt0 + jnp.log(-jnp.expm1(-dt0))
    ssm_a_log = jnp.log(jax.random.uniform(next(ks), (nc, SSM_HEADS), jnp.float32, 1.0, 16.0))
    ssm_d_skip = 1.0 + nrm((nc, SSM_HEADS), 0.02)
    ssm_norm_g = 1.0 + nrm((nc, SSM_INNER), 0.02)
    ssm_w_out = nrm((nc, SSM_INNER, D), SSM_INNER ** -0.5 * DEEPNORM_BETA)

    return {
        "x": x, "c": c, "positions": positions,
        "w_mod": w_mod, "b_mod": b_mod, "ln_g": ln_g, "ln_b": ln_b,
        "ffn_w_in": ffn_w_in, "ffn_w_out": ffn_w_out,
        "mla_w_in": mla_w_in, "mla_q_norm_g": mla_q_norm_g, "mla_kv_norm_g": mla_kv_norm_g,
        "mla_w_q_up": mla_w_q_up, "mla_w_kv_up": mla_w_kv_up, "mla_w_out": mla_w_out,
        "diff_w_in": diff_w_in, "diff_lambda_q": diff_lambda_q, "diff_lambda_k": diff_lambda_k,
        "diff_subln_g": diff_subln_g, "diff_w_out": diff_w_out,
        "ssm_w_in": ssm_w_in, "ssm_conv_w": ssm_conv_w, "ssm_conv_b": ssm_conv_b,
        "ssm_dt_bias": ssm_dt_bias, "ssm_a_log": ssm_a_log, "ssm_d_skip": ssm_d_skip,
        "ssm_norm_g": ssm_norm_g, "ssm_w_out": ssm_w_out,
    }


def reference(x, c, positions, w_mod, b_mod, ln_g, ln_b, ffn_w_in, ffn_w_out,
              mla_w_in, mla_q_norm_g, mla_kv_norm_g, mla_w_q_up, mla_w_kv_up, mla_w_out,
              diff_w_in, diff_lambda_q, diff_lambda_k, diff_subln_g, diff_w_out,
              ssm_w_in, ssm_conv_w, ssm_conv_b, ssm_dt_bias, ssm_a_log, ssm_d_skip,
              ssm_norm_g, ssm_w_out):
    cond = jax.nn.silu(c)
    for i in range(DEPTH):
        mod = (cond @ w_mod[i] + b_mod[i]).reshape(-1, N_SUBLAYERS, N_MOD, D_MODEL)
        kind, idx = i % N_MIXERS, i // N_MIXERS

        x = _sublayer(x, lambda h: _swiglu(h, ffn_w_in[i, 0], ffn_w_out[i, 0]),
                      mod[:, 0], ln_g[i, 0], ln_b[i, 0], FFN_RES_WEIGHT)

        if kind == 0:
            mixer = lambda h: _mla(h, positions, mla_w_in[idx], mla_q_norm_g[idx], mla_kv_norm_g[idx],
                                   mla_w_q_up[idx], mla_w_kv_up[idx], mla_w_out[idx])
        elif kind == 1:
            mixer = lambda h: _diff_attn(h, i, diff_w_in[idx], diff_lambda_q[idx], diff_lambda_k[idx],
                                         diff_subln_g[idx], diff_w_out[idx])
        else:
            mixer = lambda h: _mamba2(h, ssm_w_in[idx], ssm_conv_w[idx], ssm_conv_b[idx], ssm_dt_bias[idx],
                                      ssm_a_log[idx], ssm_d_skip[idx], ssm_norm_g[idx], ssm_w_out[idx])
        x = _sublayer(x, mixer, mod[:, 1], ln_g[i, 1], ln_b[i, 1], 1.0)

        x = _sublayer(x, lambda h: _swiglu(h, ffn_w_in[i, 1], ffn_w_out[i, 1]),
                      mod[:, 2], ln_g[i, 2], ln_b[i, 2], FFN_RES_WEIGHT)
    return x
```

```python
import functools
import math

import jax
import jax.numpy as jnp
from jax import lax
from jax.experimental import pallas as pl
from jax.experimental.pallas import tpu as pltpu

F32 = jnp.float32
BF16 = jnp.bfloat16

LN_EPS = 1e-5
RMS_EPS = 1e-6
FFN_RES_WEIGHT = 0.5
ROPE_THETA = 10000.0
MLA_NOPE_DIM = 128
MLA_ROPE_DIM = 64
MLA_V_DIM = 128
DIFF_HEAD_DIM = 64
SSM_HEAD_DIM = 64
SSM_STATE = 128
SSM_GROUPS = 4
SSM_CHUNK = 128
N_MOD_ROWS = 9
LANES = 128
NEG_BIG = -1e30

VMEM_LIMIT = 56 * 1024 * 1024

ROW_TILE = 512
ATTN_TILE = 256


def _params(*sem):
    return pltpu.CompilerParams(dimension_semantics=sem, vmem_limit_bytes=VMEM_LIMIT)


def _resident(shape):
    nd = len(shape)
    return pl.BlockSpec(shape, lambda *_: (0,) * nd, pipeline_mode=pl.Buffered(1))


def _dot(a, b):
    return jnp.dot(a, b, preferred_element_type=F32)


def _dot_nt(a, b):
    return lax.dot_general(a, b, (((1,), (1,)), ((), ())), preferred_element_type=F32)


def _sigmoid(v):
    return 1.0 / (1.0 + jnp.exp(-v))


def _modulated(x, mod_ref, j):
    m = mod_ref[0]
    return x * (1.0 + m[3 * j + 1:3 * j + 2]) + m[3 * j:3 * j + 1]


def _residual_layernorm(x, y, mod_ref, j, g_ref, b_ref, alpha, weight):
    gate = mod_ref[0][3 * j + 2:3 * j + 3]
    z = alpha * x + (weight * gate) * y
    mu = jnp.mean(z, axis=-1, keepdims=True)
    zc = z - mu
    var = jnp.mean(zc * zc, axis=-1, keepdims=True)
    return zc * lax.rsqrt(var + LN_EPS) * g_ref[...] + b_ref[...]


def _rms(v, eps=RMS_EPS):
    return v * lax.rsqrt(jnp.mean(v * v, axis=-1, keepdims=True) + eps)


def _adaln_kernel(c_ref, w_ref, b_ref, o_ref):
    c = c_ref[...]
    cond = (c * _sigmoid(c)).astype(BF16)
    o_ref[0] = _dot(cond, w_ref[0].astype(BF16)) + b_ref[0]


def _adaln(c, w_mod, b_mod):
    depth, d, n = w_mod.shape
    bsz = c.shape[0]
    tn = n // 8 if (n // 8) % LANES == 0 else n
    return pl.pallas_call(
        _adaln_kernel,
        out_shape=jax.ShapeDtypeStruct((depth, bsz, n), F32),
        grid=(depth, n // tn),
        in_specs=[pl.BlockSpec((bsz, d), lambda l, j: (0, 0)),
                  pl.BlockSpec((1, d, tn), lambda l, j: (l, 0, j)),
                  pl.BlockSpec((1, 1, tn), lambda l, j: (l, 0, j))],
        out_specs=pl.BlockSpec((1, bsz, tn), lambda l, j: (l, 0, j)),
        compiler_params=_params("parallel", "parallel"),
        name="adaln_mod",
    )(c, w_mod, b_mod.reshape(depth, 1, n))


def _ffn_kernel(x_ref, mod_ref, win_ref, wout_ref, g_ref, b_ref, o_ref, *, j, alpha, hidden, chunk):
    x = x_ref[...]
    h = _modulated(x, mod_ref, j).astype(BF16)
    y = None
    for lo in range(0, hidden, chunk):
        gate = _dot(h, win_ref[:, lo:lo + chunk])
        up = _dot(h, win_ref[:, hidden + lo:hidden + lo + chunk])
        act = (gate * _sigmoid(gate) * up).astype(BF16)
        part = _dot(act, wout_ref[lo:lo + chunk, :])
        y = part if y is None else y + part
    o_ref[...] = _residual_layernorm(x, y, mod_ref, j, g_ref, b_ref, alpha, FFN_RES_WEIGHT)


def _ffn(x2, mod, j, w_in, w_out, g, b, *, seq, alpha):
    t, d = x2.shape
    hidden = w_out.shape[0]
    tm = min(ROW_TILE, seq)
    per_batch = seq // tm
    chunk = hidden // 2 if (hidden // 2) % LANES == 0 else hidden
    return pl.pallas_call(
        functools.partial(_ffn_kernel, j=j, alpha=alpha, hidden=hidden, chunk=chunk),
        out_shape=jax.ShapeDtypeStruct((t, d), F32),
        grid=(t // tm,),
        in_specs=[pl.BlockSpec((tm, d), lambda i: (i, 0)),
                  pl.BlockSpec((1, N_MOD_ROWS, d), lambda i: (i // per_batch, 0, 0)),
                  _resident((d, 2 * hidden)),
                  _resident((hidden, d)),
                  _resident((1, d)),
                  _resident((1, d))],
        out_specs=pl.BlockSpec((tm, d), lambda i: (i, 0)),
        compiler_params=_params("parallel"),
        name="ffn_sublayer",
    )(x2, mod, w_in, w_out, g, b)


def _outproj_kernel(a_ref, x_ref, mod_ref, w_ref, g_ref, b_ref, o_ref, *, j, alpha):
    y = _dot(a_ref[...], w_ref[...])
    o_ref[...] = _residual_layernorm(x_ref[...], y, mod_ref, j, g_ref, b_ref, alpha, 1.0)


def _outproj(a, x2, mod, j, w, g, b, *, seq, alpha):
    t, d = x2.shape
    k = a.shape[1]
    tm = min(ROW_TILE, seq)
    per_batch = seq // tm
    return pl.pallas_call(
        functools.partial(_outproj_kernel, j=j, alpha=alpha),
        out_shape=jax.ShapeDtypeStruct((t, d), F32),
        grid=(t // tm,),
        in_specs=[pl.BlockSpec((tm, k), lambda i: (i, 0)),
                  pl.BlockSpec((tm, d), lambda i: (i, 0)),
                  pl.BlockSpec((1, N_MOD_ROWS, d), lambda i: (i // per_batch, 0, 0)),
                  _resident((k, d)),
                  _resident((1, d)),
                  _resident((1, d))],
        out_specs=pl.BlockSpec((tm, d), lambda i: (i, 0)),
        compiler_params=_params("parallel"),
        name="outproj_ln",
    )(a, x2, mod, w, g, b)


def _mla_prep_kernel(x_ref, mod_ref, pos_ref, fpat_ref, sgn_ref, win_ref, gq_ref, gkv_ref,
                     wqn_ref, wqr_ref, wkv_ref, q_ref, kn_ref, kr_ref, v_ref,
                     *, j, q_rank, kv_rank, heads):
    h = _modulated(x_ref[...], mod_ref, j).astype(BF16)
    lat = _dot(h, win_ref[...])
    cq = lat[:, :q_rank]
    ckv = lat[:, q_rank:q_rank + kv_rank]
    k_rope = lat[:, q_rank + kv_rank:q_rank + kv_rank + LANES]
    k_rope_sw = lat[:, q_rank + kv_rank + LANES:q_rank + kv_rank + 2 * LANES]
    qn = (_rms(cq) * gq_ref[...]).astype(BF16)
    kvn = (_rms(ckv) * gkv_ref[...]).astype(BF16)

    ang = pos_ref[...].astype(F32) * fpat_ref[...]
    cos = jnp.cos(ang)
    sin = jnp.sin(ang) * sgn_ref[...]
    kr_ref[...] = (k_rope * cos + k_rope_sw * sin).astype(BF16)

    q_nope = _dot(qn, wqn_ref[...])
    q_rope = _dot(qn, wqr_ref[...])
    hw = heads * LANES
    for hd in range(heads):
        lo = hd * LANES
        q_ref[:, 2 * lo:2 * lo + LANES] = q_nope[:, lo:lo + LANES].astype(BF16)
        q_ref[:, 2 * lo + LANES:2 * lo + 2 * LANES] = (
            q_rope[:, lo:lo + LANES] * cos + q_rope[:, hw + lo:hw + lo + LANES] * sin).astype(BF16)
    kv = _dot(kvn, wkv_ref[...])
    kn_ref[...] = kv[:, :hw].astype(BF16)
    v_ref[...] = kv[:, hw:].astype(BF16)


def _mla_prep(x2, mod, j, pos, w_in, gq, gkv, w_q_up, w_kv_up, *, seq):
    t, d = x2.shape
    q_rank, kv_rank = gq.shape[0], gkv.shape[0]
    qk_dim = MLA_NOPE_DIM + MLA_ROPE_DIM
    heads = w_q_up.shape[1] // qk_dim
    half = MLA_ROPE_DIM // 2
    pad = LANES - MLA_ROPE_DIM

    def swap(wr):
        return jnp.concatenate([wr[..., half:], wr[..., :half]], axis=-1)

    def padl(wr):
        return jnp.pad(wr, [(0, 0)] * (wr.ndim - 1) + [(0, pad)])

    lo = q_rank + kv_rank
    w_kr = w_in[:, lo:lo + MLA_ROPE_DIM]
    w_in_p = jnp.concatenate([w_in[:, :lo], padl(w_kr), padl(swap(w_kr))], axis=1).astype(BF16)
    wq = w_q_up.reshape(q_rank, heads, qk_dim)
    w_qn = wq[:, :, :MLA_NOPE_DIM].reshape(q_rank, heads * LANES).astype(BF16)
    wq_r = wq[:, :, MLA_NOPE_DIM:]
    w_qr = jnp.concatenate([padl(wq_r).reshape(q_rank, heads * LANES),
                            padl(swap(wq_r)).reshape(q_rank, heads * LANES)], axis=1).astype(BF16)
    wkv = w_kv_up.reshape(kv_rank, heads, MLA_NOPE_DIM + MLA_V_DIM)
    w_kv = jnp.concatenate([wkv[:, :, :MLA_NOPE_DIM].reshape(kv_rank, heads * LANES),
                            wkv[:, :, MLA_NOPE_DIM:].reshape(kv_rank, heads * LANES)], axis=1).astype(BF16)

    inv_freq = ROPE_THETA ** (-jnp.arange(0, MLA_ROPE_DIM, 2, dtype=F32) / MLA_ROPE_DIM)
    zeros = jnp.zeros((pad,), F32)
    fpat = jnp.concatenate([inv_freq, inv_freq, zeros]).reshape(1, LANES)
    sgn = jnp.concatenate([-jnp.ones((half,), F32), jnp.ones((half,), F32), zeros]).reshape(1, LANES)

    tm = min(ROW_TILE, seq)
    per_batch = seq // tm
    hw = heads * LANES
    row = lambda i: (i, 0)
    return pl.pallas_call(
        functools.partial(_mla_prep_kernel, j=j, q_rank=q_rank, kv_rank=kv_rank, heads=heads),
        out_shape=(jax.ShapeDtypeStruct((t, 2 * hw), BF16), jax.ShapeDtypeStruct((t, hw), BF16),
                   jax.ShapeDtypeStruct((t, LANES), BF16), jax.ShapeDtypeStruct((t, hw), BF16)),
        grid=(t // tm,),
        in_specs=[pl.BlockSpec((tm, d), row),
                  pl.BlockSpec((1, N_MOD_ROWS, d), lambda i: (i // per_batch, 0, 0)),
                  pl.BlockSpec((tm, 1), row),
                  _resident((1, LANES)), _resident((1, LANES)),
                  _resident(w_in_p.shape), _resident((1, q_rank)), _resident((1, kv_rank)),
                  _resident(w_qn.shape), _resident(w_qr.shape), _resident(w_kv.shape)],
        out_specs=(pl.BlockSpec((tm, 2 * hw), row), pl.BlockSpec((tm, hw), row),
                   pl.BlockSpec((tm, LANES), row), pl.BlockSpec((tm, hw), row)),
        compiler_params=_params("parallel"),
        name="mla_prep",
    )(x2, mod, pos.reshape(t, 1), fpat, sgn, w_in_p, gq.reshape(1, -1), gkv.reshape(1, -1),
      w_qn, w_qr, w_kv)


def _flash_blocks(q, load_k, load_v, qi, *, scale, tq, dv):
    rows = q.shape[0]
    r = lax.broadcasted_iota(jnp.int32, (tq, tq), 0)
    c = lax.broadcasted_iota(jnp.int32, (tq, tq), 1)
    causal = jnp.concatenate([c <= r] * (rows // tq), axis=0)

    def step(jblk, carry, masked):
        m, l, acc = carry
        s = _dot_nt(q, load_k(jblk)) * scale
        if masked:
            s = jnp.where(causal, s, NEG_BIG)
        m_new = jnp.maximum(m, jnp.max(s, axis=-1, keepdims=True))
        p = jnp.exp(s - m_new)
        corr = jnp.exp(m - m_new)
        l = corr * l + jnp.sum(p, axis=-1, keepdims=True)
        acc = corr * acc + _dot(p.astype(BF16), load_v(jblk))
        return m_new, l, acc

    init = (jnp.full((rows, 1), NEG_BIG, F32), jnp.zeros((rows, 1), F32), jnp.zeros((rows, dv), F32))
    carry = lax.fori_loop(0, qi, lambda jb, cr: step(jb, cr, False), init)
    _, l, acc = step(qi, carry, True)
    return acc, l


def _mla_attn_kernel(q_ref, kn_ref, kr_ref, v_ref, o_ref, *, scale, tq):
    qi = pl.program_id(2)

    def load_k(jb):
        ks = pl.ds(pl.multiple_of(jb * tq, tq), tq)
        return jnp.concatenate([kn_ref[ks, :], kr_ref[ks, :]], axis=1)

    def load_v(jb):
        return v_ref[pl.ds(pl.multiple_of(jb * tq, tq), tq), :]

    acc, l = _flash_blocks(q_ref[...], load_k, load_v, qi, scale=scale, tq=tq, dv=MLA_V_DIM)
    o_ref[...] = (acc / l).astype(o_ref.dtype)


def _mla_attn(q, kn, kr, v, *, batch, seq, heads):
    t = q.shape[0]
    tq = min(ATTN_TILE, seq)
    nq = seq // tq
    scale = (MLA_NOPE_DIM + MLA_ROPE_DIM) ** -0.5
    return pl.pallas_call(
        functools.partial(_mla_attn_kernel, scale=scale, tq=tq),
        out_shape=jax.ShapeDtypeStruct((t, heads * MLA_V_DIM), BF16),
        grid=(batch, heads, nq),
        in_specs=[pl.BlockSpec((tq, 2 * LANES), lambda b, h, i: (b * nq + i, h)),
                  pl.BlockSpec((seq, LANES), lambda b, h, i: (b, h)),
                  pl.BlockSpec((seq, LANES), lambda b, h, i: (b, 0)),
                  pl.BlockSpec((seq, LANES), lambda b, h, i: (b, h))],
        out_specs=pl.BlockSpec((tq, MLA_V_DIM), lambda b, h, i: (b * nq + i, h)),
        compiler_params=_params("parallel", "parallel", "arbitrary"),
        name="mla_attn",
    )(q, kn, kr, v)


def _proj_kernel(x_ref, mod_ref, w_ref, o_ref, *, j):
    h = _modulated(x_ref[...], mod_ref, j).astype(BF16)
    o_ref[...] = _dot(h, w_ref[...]).astype(o_ref.dtype)


def _proj(x2, mod, j, w, *, seq):
    t, d = x2.shape
    n = w.shape[1]
    tm = min(ROW_TILE, seq)
    per_batch = seq // tm
    return pl.pallas_call(
        functools.partial(_proj_kernel, j=j),
        out_shape=jax.ShapeDtypeStruct((t, n), BF16),
        grid=(t // tm,),
        in_specs=[pl.BlockSpec((tm, d), lambda i: (i, 0)),
                  pl.BlockSpec((1, N_MOD_ROWS, d), lambda i: (i // per_batch, 0, 0)),
                  _resident((d, n))],
        out_specs=pl.BlockSpec((tm, n), lambda i: (i, 0)),
        compiler_params=_params("parallel"),
        name="mod_proj",
    )(x2, mod, w)


def _diff_attn_kernel(q_ref, k_ref, v_ref, lq_ref, lk_ref, g_ref, o_ref, *, scale, tq, lam_init):
    qi = pl.program_id(2)
    q = q_ref[...]
    lane = lax.broadcasted_iota(jnp.int32, q.shape, 1)
    zero = jnp.zeros_like(q)
    q2 = jnp.concatenate([jnp.where(lane < DIFF_HEAD_DIM, q, zero),
                          jnp.where(lane >= DIFF_HEAD_DIM, q, zero)], axis=0)

    def load_k(jb):
        return k_ref[pl.ds(pl.multiple_of(jb * tq, tq), tq), :]

    def load_v(jb):
        return v_ref[pl.ds(pl.multiple_of(jb * tq, tq), tq), :]

    acc, l = _flash_blocks(q2, load_k, load_v, qi, scale=scale, tq=tq, dv=LANES)
    o2 = acc / l
    dots = jnp.sum(lq_ref[...] * lk_ref[...], axis=-1, keepdims=True)
    e = jnp.exp(dots)
    lam = e[0:1] - e[1:2] + lam_init
    o = o2[:tq] - lam * o2[tq:]
    o_ref[...] = (_rms(o) * g_ref[...] * (1.0 - lam_init)).astype(o_ref.dtype)


def _diff_attn(qkv, lq, lk, g, *, batch, seq, heads, lam_init):
    t = qkv.shape[0]
    tq = min(ATTN_TILE, seq)
    nq = seq // tq
    return pl.pallas_call(
        functools.partial(_diff_attn_kernel, scale=DIFF_HEAD_DIM ** -0.5, tq=tq, lam_init=lam_init),
        out_shape=jax.ShapeDtypeStruct((t, heads * LANES), BF16),
        grid=(batch, heads, nq),
        in_specs=[pl.BlockSpec((tq, LANES), lambda b, h, i: (b * nq + i, h)),
                  pl.BlockSpec((seq, LANES), lambda b, h, i: (b, heads + h)),
                  pl.BlockSpec((seq, LANES), lambda b, h, i: (b, 2 * heads + h)),
                  pl.BlockSpec(lq.shape, lambda b, h, i: (0, 0)),
                  pl.BlockSpec(lk.shape, lambda b, h, i: (0, 0)),
                  pl.BlockSpec((1, LANES), lambda b, h, i: (0, 0))],
        out_specs=pl.BlockSpec((tq, LANES), lambda b, h, i: (b * nq + i, h)),
        compiler_params=_params("parallel", "parallel", "arbitrary"),
        name="diff_attn",
    )(qkv, qkv, qkv, lq, lk, g.reshape(1, LANES))


def _ssm_prep_kernel(x_ref, mod_ref, wz_ref, wx_ref, wdt_ref, cw_ref, cb_ref, dtb_ref,
                     z_ref, xbc_ref, dt_ref, tail_ref, *, j, per_batch, taps):
    i = pl.program_id(0)
    h = _modulated(x_ref[...], mod_ref, j).astype(BF16)
    z_ref[...] = _dot(h, wz_ref[...])
    dt_raw = _dot(h, wdt_ref[...]) + dtb_ref[...]
    dt_ref[...] = jnp.maximum(dt_raw, 0.0) + jnp.log1p(jnp.exp(-jnp.abs(dt_raw)))

    xbc = _dot(h, wx_ref[...])
    tm = xbc.shape[0]

    @pl.when(i % per_batch == 0)
    def _():
        tail_ref[...] = jnp.zeros_like(tail_ref)

    cw = cw_ref[...]
    bias = cb_ref[...]
    acc = xbc * cw[taps - 1:taps] + bias
    for sh in range(1, taps):
        acc = acc + pltpu.roll(xbc, sh, 0) * cw[taps - 1 - sh:taps - sh]
    xbc_ref[...] = acc * _sigmoid(acc)

    head = xbc[:8]
    tail = tail_ref[...]
    rid = lax.broadcasted_iota(jnp.int32, head.shape, 0)
    acc8 = head * cw[taps - 1:taps] + bias
    for sh in range(1, taps):
        prev = jnp.where(rid < sh, pltpu.roll(tail, sh, 0), pltpu.roll(head, sh, 0))
        acc8 = acc8 + prev * cw[taps - 1 - sh:taps - sh]
    xbc_ref[0:8, :] = acc8 * _sigmoid(acc8)
    tail_ref[...] = xbc[tm - 8:tm]


def _ssm_prep(x2, mod, j, w_in, conv_w, conv_b, dt_bias, *, seq, inner, conv_dim, heads):
    t, d = x2.shape
    taps = conv_w.shape[0]
    w_z = w_in[:, :inner].astype(BF16)
    w_x = w_in[:, inner:inner + conv_dim].astype(BF16)
    w_dt = jnp.pad(w_in[:, inner + conv_dim:], ((0, 0), (0, LANES - heads))).astype(BF16)
    dtb = jnp.pad(dt_bias.astype(F32), (0, LANES - heads)).reshape(1, LANES)
    tm = min(ROW_TILE, seq)
    per_batch = seq // tm
    row = lambda i: (i, 0)
    return pl.pallas_call(
        functools.partial(_ssm_prep_kernel, j=j, per_batch=per_batch, taps=taps),
        out_shape=(jax.ShapeDtypeStruct((t, inner), F32), jax.ShapeDtypeStruct((t, conv_dim), F32),
                   jax.ShapeDtypeStruct((t, LANES), F32)),
        grid=(t // tm,),
        in_specs=[pl.BlockSpec((tm, d), row),
                  pl.BlockSpec((1, N_MOD_ROWS, d), lambda i: (i // per_batch, 0, 0)),
                  _resident(w_z.shape), _resident(w_x.shape), _resident(w_dt.shape),
                  _resident((taps, conv_dim)), _resident((1, conv_dim)), _resident((1, LANES))],
        out_specs=(pl.BlockSpec((tm, inner), row), pl.BlockSpec((tm, conv_dim), row),
                   pl.BlockSpec((tm, LANES), row)),
        scratch_shapes=[pltpu.VMEM((8, conv_dim), F32)],
        compiler_params=_params("arbitrary"),
        name="ssm_prep",
    )(x2, mod, w_z, w_x, w_dt, conv_w, conv_b.reshape(1, conv_dim), dtb)


def _ssd_kernel(xbc_ref, dt_ref, z_ref, alog_ref, dskip_ref, ng_ref, y_ref, state_ref, ybuf_ref,
                *, inner, groups, nstate, hdim):
    c = pl.program_id(1)

    @pl.when(c == 0)
    def _():
        state_ref[...] = jnp.zeros_like(state_ref)

    L = dt_ref.shape[0]
    gw = inner // groups
    pairs_per_group = gw // LANES
    heads_per_group = gw // hdim

    dt = dt_ref[...]
    da = dt * (-jnp.exp(alog_ref[...]))
    ri = lax.broadcasted_iota(jnp.int32, (L, L), 0)
    ci = lax.broadcasted_iota(jnp.int32, (L, L), 1)
    causal = ci <= ri
    tril = causal.astype(F32)
    acum = jnp.dot(tril, da, preferred_element_type=F32, precision=lax.Precision.HIGHEST)
    acum_t = acum.T
    dt_t = dt.T
    w_t = jnp.exp(acum_t[:, L - 1:L] - acum_t) * dt_t
    lane = lax.broadcasted_iota(jnp.int32, (L, LANES), 1)
    first = lane < hdim
    first_n = lax.broadcasted_iota(jnp.int32, (nstate, LANES), 1) < hdim

    for g in range(groups):
        b_g = xbc_ref[:, inner + g * nstate:inner + (g + 1) * nstate]
        c_g = xbc_ref[:, inner + (groups + g) * nstate:inner + (groups + g + 1) * nstate]
        c_bf = c_g.astype(BF16)
        cb = _dot_nt(c_bf, b_g.astype(BF16))
        b_t = b_g.T
        y_inter = _dot(c_bf, state_ref[:, g * gw:(g + 1) * gw].astype(BF16))
        for pr in range(pairs_per_group):
            lo = g * gw + pr * LANES
            x_pair = xbc_ref[:, lo:lo + LANES]
            x_bf = x_pair.astype(BF16)
            ms, bws, eas, decs = [], [], [], []
            for sub in range(2):
                hd = g * heads_per_group + 2 * pr + sub
                col = jnp.broadcast_to(acum[:, hd:hd + 1], (L, LANES))
                seg = col - acum_t[hd:hd + 1, :]
                decay = jnp.exp(jnp.where(causal, seg, NEG_BIG))
                ms.append((cb * decay * dt_t[hd:hd + 1, :]).astype(BF16))
                bws.append((b_t * w_t[hd:hd + 1, :]).astype(BF16))
                eas.append(jnp.exp(col))
                decs.append(jnp.exp(col[L - 1:L, :]))
            y2 = _dot(jnp.concatenate(ms, axis=0), x_bf)
            y_intra = jnp.where(first, y2[:L], y2[L:])
            ea = jnp.where(first, eas[0], eas[1])
            ybuf_ref[:, lo:lo + LANES] = y_intra + y_inter[:, pr * LANES:(pr + 1) * LANES] * ea
            s2 = _dot(jnp.concatenate(bws, axis=0), x_bf)
            s_new = jnp.where(first_n, s2[:nstate], s2[nstate:])
            dec = jnp.where(first[:1], decs[0], decs[1])
            state_ref[:, lo:lo + LANES] = state_ref[:, lo:lo + LANES] * dec + s_new

    xs = xbc_ref[:, :inner]
    z = z_ref[...]
    y = (ybuf_ref[...] + dskip_ref[...] * xs) * (z * _sigmoid(z))
    for g in range(groups):
        yg = y[:, g * gw:(g + 1) * gw]
        y_ref[:, g * gw:(g + 1) * gw] = (_rms(yg) * ng_ref[:, g * gw:(g + 1) * gw]).astype(y_ref.dtype)


def _ssd(xbc, dt, z, a_log, d_skip, norm_g, *, batch, seq, inner, heads):
    t, conv_dim = xbc.shape
    L = min(SSM_CHUNK, seq)
    nc = seq // L
    hdim = inner // heads
    alog = jnp.pad(a_log.astype(F32), (0, LANES - heads)).reshape(1, LANES)
    dskip = jnp.repeat(d_skip.astype(F32), hdim).reshape(1, inner)
    row = lambda b, c: (b * nc + c, 0)
    return pl.pallas_call(
        functools.partial(_ssd_kernel, inner=inner, groups=SSM_GROUPS, nstate=SSM_STATE, hdim=hdim),
        out_shape=jax.ShapeDtypeStruct((t, inner), BF16),
        grid=(batch, nc),
        in_specs=[pl.BlockSpec((L, conv_dim), row), pl.BlockSpec((L, LANES), row),
                  pl.BlockSpec((L, inner), row),
                  pl.BlockSpec((1, LANES), lambda b, c: (0, 0)),
                  pl.BlockSpec((1, inner), lambda b, c: (0, 0)),
                  pl.BlockSpec((1, inner), lambda b, c: (0, 0))],
        out_specs=pl.BlockSpec((L, inner), row),
        scratch_shapes=[pltpu.VMEM((SSM_STATE, inner), F32), pltpu.VMEM((L, inner), F32)],
        compiler_params=_params("parallel", "arbitrary"),
        name="ssd_scan",
    )(xbc, dt, z, alog, dskip, norm_g.reshape(1, inner))


def kernel(x, c, positions, w_mod, b_mod, ln_g, ln_b, ffn_w_in, ffn_w_out, mla_w_in, mla_q_norm_g, mla_kv_norm_g, mla_w_q_up, mla_w_kv_up, mla_w_out, diff_w_in, diff_lambda_q, diff_lambda_k, diff_subln_g, diff_w_out, ssm_w_in, ssm_conv_w, ssm_conv_b, ssm_dt_bias, ssm_a_log, ssm_d_skip, ssm_norm_g, ssm_w_out):
    batch, seq, d = x.shape
    depth = w_mod.shape[0]
    alpha = (2.0 * depth) ** 0.25
    t = batch * seq
    mod_all = _adaln(c, w_mod, b_mod).reshape(depth, batch, N_MOD_ROWS, d)
    x2 = x.reshape(t, d)
    lng = ln_g.reshape(depth, 3, 1, d)
    lnb = ln_b.reshape(depth, 3, 1, d)
    ffn_in = ffn_w_in.astype(BF16)
    ffn_out = ffn_w_out.astype(BF16)

    for i in range(depth):
        mod = mod_all[i]
        kind, idx = i % 3, i // 3
        x2 = _ffn(x2, mod, 0, ffn_in[i, 0], ffn_out[i, 0], lng[i, 0], lnb[i, 0], seq=seq, alpha=alpha)
        if kind == 0:
            heads = mla_w_out.shape[1] // MLA_V_DIM
            q, kn, kr, v = _mla_prep(x2, mod, 1, positions, mla_w_in[idx], mla_q_norm_g[idx],
                                     mla_kv_norm_g[idx], mla_w_q_up[idx], mla_w_kv_up[idx], seq=seq)
            a = _mla_attn(q, kn, kr, v, batch=batch, seq=seq, heads=heads)
            w_o = mla_w_out[idx]
        elif kind == 1:
            heads = d // (2 * DIFF_HEAD_DIM)
            lam_init = 0.8 - 0.6 * math.exp(-0.3 * i)
            qkv = _proj(x2, mod, 1, diff_w_in[idx].astype(BF16), seq=seq)
            a = _diff_attn(qkv, diff_lambda_q[idx], diff_lambda_k[idx], diff_subln_g[idx],
                           batch=batch, seq=seq, heads=heads, lam_init=lam_init)
            w_o = diff_w_out[idx]
        else:
            heads = ssm_a_log.shape[1]
            inner = ssm_norm_g.shape[1]
            conv_dim = ssm_conv_w.shape[2]
            z, xbc, dt = _ssm_prep(x2, mod, 1, ssm_w_in[idx], ssm_conv_w[idx], ssm_conv_b[idx],
                                   ssm_dt_bias[idx], seq=seq, inner=inner, conv_dim=conv_dim, heads=heads)
            a = _ssd(xbc, dt, z, ssm_a_log[idx], ssm_d_skip[idx], ssm_norm_g[idx],
                     batch=batch, seq=seq, inner=inner, heads=heads)
            w_o = ssm_w_out[idx]
        x2 = _outproj(a, x2, mod, 1, w_o.astype(BF16), lng[i, 1], lnb[i, 1], seq=seq, alpha=alpha)
        x2 = _ffn(x2, mod, 2, ffn_in[i, 1], ffn_out[i, 1], lng[i, 2], lnb[i, 2], seq=seq, alpha=alpha)
    return x2.reshape(batch, seq, d)
```

```python
import functools
import math

import jax
import jax.numpy as jnp
from jax import lax
from jax.experimental import pallas as pl
from jax.experimental.pallas import tpu as pltpu

F32 = jnp.float32
BF16 = jnp.bfloat16

LN_EPS = 1e-5
RMS_EPS = 1e-6
FFN_RES_WEIGHT = 0.5
ROPE_THETA = 10000.0
MLA_NOPE_DIM = 128
MLA_ROPE_DIM = 64
MLA_V_DIM = 128
DIFF_HEAD_DIM = 64
SSM_HEAD_DIM = 64
SSM_STATE = 128
SSM_GROUPS = 4
SSM_CHUNK = 128
N_MOD_ROWS = 9
LANES = 128
NEG_BIG = -1e30
LOG2E = 1.4426950408889634

VMEM_LIMIT = 56 * 1024 * 1024

ROW_TILE = 512
ATTN_TILE = 256
DIFF_HEADS_PER_STEP = 4


def _params(*sem):
    return pltpu.CompilerParams(dimension_semantics=sem, vmem_limit_bytes=VMEM_LIMIT)


def _resident(shape):
    nd = len(shape)
    return pl.BlockSpec(shape, lambda *_: (0,) * nd, pipeline_mode=pl.Buffered(1))


def _dot(a, b):
    return jnp.dot(a, b, preferred_element_type=F32)


def _dot_nt(a, b):
    return lax.dot_general(a, b, (((1,), (1,)), ((), ())), preferred_element_type=F32)


def _sigmoid(v):
    return 1.0 / (1.0 + jnp.exp(-v))


def _modulated(x, mod_ref, j):
    m = mod_ref[0]
    return x * (1.0 + m[3 * j + 1:3 * j + 2]) + m[3 * j:3 * j + 1]


def _residual_layernorm(x, y, mod_ref, j, g_ref, b_ref, alpha, weight):
    gate = mod_ref[0][3 * j + 2:3 * j + 3]
    z = alpha * x + (weight * gate) * y
    mu = jnp.mean(z, axis=-1, keepdims=True)
    zc = z - mu
    var = jnp.mean(zc * zc, axis=-1, keepdims=True)
    return zc * lax.rsqrt(var + LN_EPS) * g_ref[...] + b_ref[...]


def _rms(v, eps=RMS_EPS):
    return v * lax.rsqrt(jnp.mean(v * v, axis=-1, keepdims=True) + eps)


def _adaln_kernel(c_ref, w_ref, b_ref, o_ref):
    c = c_ref[...]
    cond = (c * _sigmoid(c)).astype(BF16)
    o_ref[0] = _dot(cond, w_ref[0].astype(BF16)) + b_ref[0]


def _adaln(c, w_mod, b_mod):
    depth, d, n = w_mod.shape
    bsz = c.shape[0]
    tn = n // 8 if (n // 8) % LANES == 0 else n
    return pl.pallas_call(
        _adaln_kernel,
        out_shape=jax.ShapeDtypeStruct((depth, bsz, n), F32),
        grid=(depth, n // tn),
        in_specs=[pl.BlockSpec((bsz, d), lambda l, j: (0, 0)),
                  pl.BlockSpec((1, d, tn), lambda l, j: (l, 0, j)),
                  pl.BlockSpec((1, 1, tn), lambda l, j: (l, 0, j))],
        out_specs=pl.BlockSpec((1, bsz, tn), lambda l, j: (l, 0, j)),
        compiler_params=_params("parallel", "parallel"),
        name="adaln_mod",
    )(c, w_mod, b_mod.reshape(depth, 1, n))


def _ffn_kernel(x_ref, mod_ref, win_ref, wout_ref, g_ref, b_ref, o_ref, *, j, alpha, hidden, chunk):
    x = x_ref[...]
    h = _modulated(x, mod_ref, j).astype(BF16)
    y = None
    for lo in range(0, hidden, chunk):
        gate = _dot(h, win_ref[:, lo:lo + chunk])
        up = _dot(h, win_ref[:, hidden + lo:hidden + lo + chunk])
        act = (gate * _sigmoid(gate) * up).astype(BF16)
        part = _dot(act, wout_ref[lo:lo + chunk, :])
        y = part if y is None else y + part
    o_ref[...] = _residual_layernorm(x, y, mod_ref, j, g_ref, b_ref, alpha, FFN_RES_WEIGHT)


def _ffn(x2, mod, j, w_in, w_out, g, b, *, seq, alpha):
    t, d = x2.shape
    hidden = w_out.shape[0]
    tm = min(ROW_TILE, seq)
    per_batch = seq // tm
    chunk = hidden // 2 if (hidden // 2) % LANES == 0 else hidden
    return pl.pallas_call(
        functools.partial(_ffn_kernel, j=j, alpha=alpha, hidden=hidden, chunk=chunk),
        out_shape=jax.ShapeDtypeStruct((t, d), F32),
        grid=(t // tm,),
        in_specs=[pl.BlockSpec((tm, d), lambda i: (i, 0)),
                  pl.BlockSpec((1, N_MOD_ROWS, d), lambda i: (i // per_batch, 0, 0)),
                  _resident((d, 2 * hidden)),
                  _resident((hidden, d)),
                  _resident((1, d)),
                  _resident((1, d))],
        out_specs=pl.BlockSpec((tm, d), lambda i: (i, 0)),
        compiler_params=_params("parallel"),
        name="ffn_sublayer",
    )(x2, mod, w_in, w_out, g, b)


def _outproj_kernel(a_ref, x_ref, mod_ref, w_ref, g_ref, b_ref, o_ref, *, j, alpha):
    y = _dot(a_ref[...], w_ref[...])
    o_ref[...] = _residual_layernorm(x_ref[...], y, mod_ref, j, g_ref, b_ref, alpha, 1.0)


def _outproj(a, x2, mod, j, w, g, b, *, seq, alpha):
    t, d = x2.shape
    k = a.shape[1]
    tm = min(ROW_TILE, seq)
    per_batch = seq // tm
    return pl.pallas_call(
        functools.partial(_outproj_kernel, j=j, alpha=alpha),
        out_shape=jax.ShapeDtypeStruct((t, d), F32),
        grid=(t // tm,),
        in_specs=[pl.BlockSpec((tm, k), lambda i: (i, 0)),
                  pl.BlockSpec((tm, d), lambda i: (i, 0)),
                  pl.BlockSpec((1, N_MOD_ROWS, d), lambda i: (i // per_batch, 0, 0)),
                  _resident((k, d)),
                  _resident((1, d)),
                  _resident((1, d))],
        out_specs=pl.BlockSpec((tm, d), lambda i: (i, 0)),
        compiler_params=_params("parallel"),
        name="outproj_ln",
    )(a, x2, mod, w, g, b)


def _mla_prep_kernel(x_ref, mod_ref, pos_ref, fpat_ref, sgn_ref, win_ref, gq_ref, gkv_ref,
                     wqn_ref, wqr_ref, wkv_ref, q_ref, kn_ref, kr_ref, v_ref,
                     *, j, q_rank, kv_rank, heads, q_scale):
    h = _modulated(x_ref[...], mod_ref, j).astype(BF16)
    lat = _dot(h, win_ref[...])
    cq = lat[:, :q_rank]
    ckv = lat[:, q_rank:q_rank + kv_rank]
    k_rope = lat[:, q_rank + kv_rank:q_rank + kv_rank + LANES]
    k_rope_sw = lat[:, q_rank + kv_rank + LANES:q_rank + kv_rank + 2 * LANES]
    qn = (_rms(cq) * gq_ref[...]).astype(BF16)
    kvn = (_rms(ckv) * gkv_ref[...]).astype(BF16)

    ang = pos_ref[...].astype(F32) * fpat_ref[...]
    cos = jnp.cos(ang)
    sin = jnp.sin(ang) * sgn_ref[...]
    kr_ref[...] = (k_rope * cos + k_rope_sw * sin).astype(BF16)

    q_nope = _dot(qn, wqn_ref[...]) * q_scale
    q_rope = _dot(qn, wqr_ref[...])
    cos_q = cos * q_scale
    sin_q = sin * q_scale
    hw = heads * LANES
    for hd in range(heads):
        lo = hd * LANES
        q_ref[:, 2 * lo:2 * lo + LANES] = q_nope[:, lo:lo + LANES].astype(BF16)
        q_ref[:, 2 * lo + LANES:2 * lo + 2 * LANES] = (
            q_rope[:, lo:lo + LANES] * cos_q + q_rope[:, hw + lo:hw + lo + LANES] * sin_q).astype(BF16)
    kv = _dot(kvn, wkv_ref[...])
    kn_ref[...] = kv[:, :hw].astype(BF16)
    v_ref[...] = kv[:, hw:].astype(BF16)


def _mla_prep(x2, mod, j, pos, w_in, gq, gkv, w_q_up, w_kv_up, *, seq):
    t, d = x2.shape
    q_rank, kv_rank = gq.shape[0], gkv.shape[0]
    qk_dim = MLA_NOPE_DIM + MLA_ROPE_DIM
    heads = w_q_up.shape[1] // qk_dim
    half = MLA_ROPE_DIM // 2
    pad = LANES - MLA_ROPE_DIM

    def swap(wr):
        return jnp.concatenate([wr[..., half:], wr[..., :half]], axis=-1)

    def padl(wr):
        return jnp.pad(wr, [(0, 0)] * (wr.ndim - 1) + [(0, pad)])

    lo = q_rank + kv_rank
    w_kr = w_in[:, lo:lo + MLA_ROPE_DIM]
    w_in_p = jnp.concatenate([w_in[:, :lo], padl(w_kr), padl(swap(w_kr))], axis=1).astype(BF16)
    wq = w_q_up.reshape(q_rank, heads, qk_dim)
    w_qn = wq[:, :, :MLA_NOPE_DIM].reshape(q_rank, heads * LANES).astype(BF16)
    wq_r = wq[:, :, MLA_NOPE_DIM:]
    w_qr = jnp.concatenate([padl(wq_r).reshape(q_rank, heads * LANES),
                            padl(swap(wq_r)).reshape(q_rank, heads * LANES)], axis=1).astype(BF16)
    wkv = w_kv_up.reshape(kv_rank, heads, MLA_NOPE_DIM + MLA_V_DIM)
    w_kv = jnp.concatenate([wkv[:, :, :MLA_NOPE_DIM].reshape(kv_rank, heads * LANES),
                            wkv[:, :, MLA_NOPE_DIM:].reshape(kv_rank, heads * LANES)], axis=1).astype(BF16)

    inv_freq = ROPE_THETA ** (-jnp.arange(0, MLA_ROPE_DIM, 2, dtype=F32) / MLA_ROPE_DIM)
    zeros = jnp.zeros((pad,), F32)
    fpat = jnp.concatenate([inv_freq, inv_freq, zeros]).reshape(1, LANES)
    sgn = jnp.concatenate([-jnp.ones((half,), F32), jnp.ones((half,), F32), zeros]).reshape(1, LANES)

    tm = min(ROW_TILE, seq)
    per_batch = seq // tm
    hw = heads * LANES
    row = lambda i: (i, 0)
    return pl.pallas_call(
        functools.partial(_mla_prep_kernel, j=j, q_rank=q_rank, kv_rank=kv_rank, heads=heads,
                          q_scale=qk_dim ** -0.5 * LOG2E),
        out_shape=(jax.ShapeDtypeStruct((t, 2 * hw), BF16), jax.ShapeDtypeStruct((t, hw), BF16),
                   jax.ShapeDtypeStruct((t, LANES), BF16), jax.ShapeDtypeStruct((t, hw), BF16)),
        grid=(t // tm,),
        in_specs=[pl.BlockSpec((tm, d), row),
                  pl.BlockSpec((1, N_MOD_ROWS, d), lambda i: (i // per_batch, 0, 0)),
                  pl.BlockSpec((tm, 1), row),
                  _resident((1, LANES)), _resident((1, LANES)),
                  _resident(w_in_p.shape), _resident((1, q_rank)), _resident((1, kv_rank)),
                  _resident(w_qn.shape), _resident(w_qr.shape), _resident(w_kv.shape)],
        out_specs=(pl.BlockSpec((tm, 2 * hw), row), pl.BlockSpec((tm, hw), row),
                   pl.BlockSpec((tm, LANES), row), pl.BlockSpec((tm, hw), row)),
        compiler_params=_params("parallel"),
        name="mla_prep",
    )(x2, mod, pos.reshape(t, 1), fpat, sgn, w_in_p, gq.reshape(1, -1), gkv.reshape(1, -1),
      w_qn, w_qr, w_kv)


def _causal_softmax_pv(queries, load_k, load_v, qi, s_ref, m_ref, l_ref, acc_ref, *, tq):
    n = len(queries)
    rows = queries[0].shape[0]
    groups = tq // LANES
    r = lax.broadcasted_iota(jnp.int32, (tq, tq), 0)
    col = lax.broadcasted_iota(jnp.int32, (tq, tq), 1)
    causal = jnp.concatenate([col <= r] * (rows // tq), axis=0)

    def lane_fold(x, op):
        out = x[:, :LANES]
        for g in range(1, groups):
            out = op(out, x[:, g * LANES:(g + 1) * LANES])
        return out

    def scores(c, jb, masked):
        s = _dot_nt(queries[c], load_k(c, jb))
        if masked:
            s = jnp.where(causal, s, NEG_BIG)
        s_ref[c, jb] = s
        return lane_fold(s, jnp.maximum)

    for c in range(n):
        m_ref[c] = scores(c, qi, True)

    def pass1(jb, carry):
        for c in range(n):
            m_ref[c] = jnp.maximum(m_ref[c], scores(c, jb, False))
        return carry

    lax.fori_loop(0, qi, pass1, 0)

    for c in range(n):
        m_ref[c] = jnp.broadcast_to(jnp.max(m_ref[c], axis=-1, keepdims=True), (rows, LANES))

    def probs(c, jb):
        s = s_ref[c, jb]
        mb = m_ref[c]
        ps = [jnp.exp2(s[:, g * LANES:(g + 1) * LANES] - mb) for g in range(groups)]
        lsum = ps[0]
        for g in range(1, groups):
            lsum = lsum + ps[g]
        p = jnp.concatenate(ps, axis=1).astype(BF16)
        return lsum, _dot(p, load_v(c, jb))

    for c in range(n):
        l_ref[c], acc_ref[c] = probs(c, qi)

    def pass2(jb, carry):
        for c in range(n):
            lsum, pv = probs(c, jb)
            l_ref[c] += lsum
            acc_ref[c] += pv
        return carry

    lax.fori_loop(0, qi, pass2, 0)
    return [acc_ref[c] / jnp.sum(l_ref[c], axis=-1, keepdims=True) for c in range(n)]


def _attn_scratch(chains, nkv, rows, tq, dv):
    return [pltpu.VMEM((chains, nkv, rows, tq), F32), pltpu.VMEM((chains, rows, LANES), F32),
            pltpu.VMEM((chains, rows, LANES), F32), pltpu.VMEM((chains, rows, dv), F32)]


def _mla_attn_kernel(q_ref, kn_ref, kr_ref, v_ref, o_ref, s_ref, m_ref, l_ref, acc_ref, *, tq, heads):
    qi = pl.program_id(1)

    def rows_of(jb):
        return pl.ds(pl.multiple_of(jb * tq, tq), tq)

    def load_k(c, jb):
        return jnp.concatenate([kn_ref[rows_of(jb), c * LANES:(c + 1) * LANES], kr_ref[rows_of(jb), :]], axis=1)

    def load_v(c, jb):
        return v_ref[rows_of(jb), c * MLA_V_DIM:(c + 1) * MLA_V_DIM]

    queries = [q_ref[:, 2 * c * LANES:2 * (c + 1) * LANES] for c in range(heads)]
    outs = _causal_softmax_pv(queries, load_k, load_v, qi, s_ref, m_ref, l_ref, acc_ref, tq=tq)
    for c in range(heads):
        o_ref[:, c * MLA_V_DIM:(c + 1) * MLA_V_DIM] = outs[c].astype(o_ref.dtype)


def _mla_attn(q, kn, kr, v, *, batch, seq, heads):
    t = q.shape[0]
    tq = min(ATTN_TILE, seq)
    nq = seq // tq
    hw = heads * LANES
    return pl.pallas_call(
        functools.partial(_mla_attn_kernel, tq=tq, heads=heads),
        out_shape=jax.ShapeDtypeStruct((t, heads * MLA_V_DIM), BF16),
        grid=(batch, nq),
        in_specs=[pl.BlockSpec((tq, 2 * hw), lambda b, i: (b * nq + i, 0)),
                  pl.BlockSpec((seq, hw), lambda b, i: (b, 0)),
                  pl.BlockSpec((seq, LANES), lambda b, i: (b, 0)),
                  pl.BlockSpec((seq, hw), lambda b, i: (b, 0))],
        out_specs=pl.BlockSpec((tq, heads * MLA_V_DIM), lambda b, i: (b * nq + i, 0)),
        scratch_shapes=_attn_scratch(heads, nq, tq, tq, MLA_V_DIM),
        compiler_params=_params("parallel", "arbitrary"),
        name="mla_attn",
    )(q, kn, kr, v)


def _proj_kernel(x_ref, mod_ref, w_ref, cs_ref, o_ref, *, j):
    h = _modulated(x_ref[...], mod_ref, j).astype(BF16)
    o_ref[...] = (_dot(h, w_ref[...]) * cs_ref[...]).astype(o_ref.dtype)


def _proj(x2, mod, j, w, col_scale, *, seq):
    t, d = x2.shape
    n = w.shape[1]
    tm = min(ROW_TILE, seq)
    per_batch = seq // tm
    return pl.pallas_call(
        functools.partial(_proj_kernel, j=j),
        out_shape=jax.ShapeDtypeStruct((t, n), BF16),
        grid=(t // tm,),
        in_specs=[pl.BlockSpec((tm, d), lambda i: (i, 0)),
                  pl.BlockSpec((1, N_MOD_ROWS, d), lambda i: (i // per_batch, 0, 0)),
                  _resident((d, n)), _resident((1, n))],
        out_specs=pl.BlockSpec((tm, n), lambda i: (i, 0)),
        compiler_params=_params("parallel"),
        name="mod_proj",
    )(x2, mod, w, col_scale)


def _diff_attn_kernel(q_ref, k_ref, v_ref, lq_ref, lk_ref, g_ref, o_ref, s_ref, m_ref, l_ref, acc_ref,
                      *, tq, heads, lam_init):
    qi = pl.program_id(2)

    def rows_of(jb):
        return pl.ds(pl.multiple_of(jb * tq, tq), tq)

    def load_kv(ref):
        return lambda c, jb: ref[rows_of(jb), c * LANES:(c + 1) * LANES]

    lane = lax.broadcasted_iota(jnp.int32, (tq, LANES), 1)
    queries = []
    for c in range(heads):
        q = q_ref[:, c * LANES:(c + 1) * LANES]
        zero = jnp.zeros_like(q)
        queries.append(jnp.concatenate([jnp.where(lane < DIFF_HEAD_DIM, q, zero),
                                        jnp.where(lane >= DIFF_HEAD_DIM, q, zero)], axis=0))
    outs = _causal_softmax_pv(queries, load_kv(k_ref), load_kv(v_ref), qi, s_ref, m_ref, l_ref, acc_ref, tq=tq)

    dots = jnp.sum(lq_ref[...] * lk_ref[...], axis=-1, keepdims=True)
    e = jnp.exp(dots)
    lam = e[0:1] - e[1:2] + lam_init
    for c in range(heads):
        o = outs[c][:tq] - lam * outs[c][tq:]
        o_ref[:, c * LANES:(c + 1) * LANES] = (_rms(o) * g_ref[...] * (1.0 - lam_init)).astype(o_ref.dtype)


def _diff_attn(qkv, lq, lk, g, *, batch, seq, heads, lam_init):
    t = qkv.shape[0]
    tq = min(ATTN_TILE, seq)
    nq = seq // tq
    hpg = min(DIFF_HEADS_PER_STEP, heads)
    ng = heads // hpg
    gw = hpg * LANES
    return pl.pallas_call(
        functools.partial(_diff_attn_kernel, tq=tq, heads=hpg, lam_init=lam_init),
        out_shape=jax.ShapeDtypeStruct((t, heads * LANES), BF16),
        grid=(batch, ng, nq),
        in_specs=[pl.BlockSpec((tq, gw), lambda b, h, i: (b * nq + i, h)),
                  pl.BlockSpec((seq, gw), lambda b, h, i: (b, ng + h)),
                  pl.BlockSpec((seq, gw), lambda b, h, i: (b, 2 * ng + h)),
                  pl.BlockSpec(lq.shape, lambda b, h, i: (0, 0)),
                  pl.BlockSpec(lk.shape, lambda b, h, i: (0, 0)),
                  pl.BlockSpec((1, LANES), lambda b, h, i: (0, 0))],
        out_specs=pl.BlockSpec((tq, gw), lambda b, h, i: (b * nq + i, h)),
        scratch_shapes=_attn_scratch(hpg, nq, 2 * tq, tq, LANES),
        compiler_params=_params("parallel", "parallel", "arbitrary"),
        name="diff_attn",
    )(qkv, qkv, qkv, lq, lk, g.reshape(1, LANES))


def _ssm_prep_kernel(x_ref, mod_ref, wz_ref, wx_ref, wdt_ref, cw_ref, cb_ref, dtb_ref,
                     z_ref, xbc_ref, dt_ref, tail_ref, *, j, per_batch, taps):
    i = pl.program_id(0)
    h = _modulated(x_ref[...], mod_ref, j).astype(BF16)
    z_ref[...] = _dot(h, wz_ref[...])
    dt_raw = _dot(h, wdt_ref[...]) + dtb_ref[...]
    dt_ref[...] = jnp.maximum(dt_raw, 0.0) + jnp.log1p(jnp.exp(-jnp.abs(dt_raw)))

    xbc = _dot(h, wx_ref[...])
    tm = xbc.shape[0]

    @pl.when(i % per_batch == 0)
    def _():
        tail_ref[...] = jnp.zeros_like(tail_ref)

    cw = cw_ref[...]
    bias = cb_ref[...]
    acc = xbc * cw[taps - 1:taps] + bias
    for sh in range(1, taps):
        acc = acc + pltpu.roll(xbc, sh, 0) * cw[taps - 1 - sh:taps - sh]
    xbc_ref[...] = acc * _sigmoid(acc)

    head = xbc[:8]
    tail = tail_ref[...]
    rid = lax.broadcasted_iota(jnp.int32, head.shape, 0)
    acc8 = head * cw[taps - 1:taps] + bias
    for sh in range(1, taps):
        prev = jnp.where(rid < sh, pltpu.roll(tail, sh, 0), pltpu.roll(head, sh, 0))
        acc8 = acc8 + prev * cw[taps - 1 - sh:taps - sh]
    xbc_ref[0:8, :] = acc8 * _sigmoid(acc8)
    tail_ref[...] = xbc[tm - 8:tm]


def _ssm_prep(x2, mod, j, w_in, conv_w, conv_b, dt_bias, *, seq, inner, conv_dim, heads):
    t, d = x2.shape
    taps = conv_w.shape[0]
    w_z = w_in[:, :inner].astype(BF16)
    w_x = w_in[:, inner:inner + conv_dim].astype(BF16)
    w_dt = jnp.pad(w_in[:, inner + conv_dim:], ((0, 0), (0, LANES - heads))).astype(BF16)
    dtb = jnp.pad(dt_bias.astype(F32), (0, LANES - heads)).reshape(1, LANES)
    tm = min(ROW_TILE, seq)
    per_batch = seq // tm
    row = lambda i: (i, 0)
    return pl.pallas_call(
        functools.partial(_ssm_prep_kernel, j=j, per_batch=per_batch, taps=taps),
        out_shape=(jax.ShapeDtypeStruct((t, inner), F32), jax.ShapeDtypeStruct((t, conv_dim), F32),
                   jax.ShapeDtypeStruct((t, LANES), F32)),
        grid=(t // tm,),
        in_specs=[pl.BlockSpec((tm, d), row),
                  pl.BlockSpec((1, N_MOD_ROWS, d), lambda i: (i // per_batch, 0, 0)),
                  _resident(w_z.shape), _resident(w_x.shape), _resident(w_dt.shape),
                  _resident((taps, conv_dim)), _resident((1, conv_dim)), _resident((1, LANES))],
        out_specs=(pl.BlockSpec((tm, inner), row), pl.BlockSpec((tm, conv_dim), row),
                   pl.BlockSpec((tm, LANES), row)),
        scratch_shapes=[pltpu.VMEM((8, conv_dim), F32)],
        compiler_params=_params("arbitrary"),
        name="ssm_prep",
    )(x2, mod, w_z, w_x, w_dt, conv_w, conv_b.reshape(1, conv_dim), dtb)


def _ssd_kernel(xbc_ref, dt_ref, z_ref, alog_ref, dskip_ref, ng_ref, y_ref, state_ref, ybuf_ref,
                *, inner, groups, nstate, hdim):
    c = pl.program_id(1)

    @pl.when(c == 0)
    def _():
        state_ref[...] = jnp.zeros_like(state_ref)

    L = dt_ref.shape[0]
    gw = inner // groups
    pairs_per_group = gw // LANES
    heads_per_group = gw // hdim

    dt = dt_ref[...]
    da = dt * (-jnp.exp(alog_ref[...]))
    ri = lax.broadcasted_iota(jnp.int32, (L, L), 0)
    ci = lax.broadcasted_iota(jnp.int32, (L, L), 1)
    causal = ci <= ri
    tril = causal.astype(F32)
    acum = jnp.dot(tril, da, preferred_element_type=F32, precision=lax.Precision.HIGHEST)
    acum_t = acum.T
    dt_t = dt.T
    w_t = jnp.exp(acum_t[:, L - 1:L] - acum_t) * dt_t
    lane = lax.broadcasted_iota(jnp.int32, (L, LANES), 1)
    first = lane < hdim
    first_n = lax.broadcasted_iota(jnp.int32, (nstate, LANES), 1) < hdim

    for g in range(groups):
        b_g = xbc_ref[:, inner + g * nstate:inner + (g + 1) * nstate]
        c_g = xbc_ref[:, inner + (groups + g) * nstate:inner + (groups + g + 1) * nstate]
        c_bf = c_g.astype(BF16)
        cb = _dot_nt(c_bf, b_g.astype(BF16))
        b_t = b_g.T
        y_inter = _dot(c_bf, state_ref[:, g * gw:(g + 1) * gw].astype(BF16))
        for pr in range(pairs_per_group):
            lo = g * gw + pr * LANES
            x_pair = xbc_ref[:, lo:lo + LANES]
            x_bf = x_pair.astype(BF16)
            ms, bws, eas, decs = [], [], [], []
            for sub in range(2):
                hd = g * heads_per_group + 2 * pr + sub
                col = jnp.broadcast_to(acum[:, hd:hd + 1], (L, LANES))
                seg = col - acum_t[hd:hd + 1, :]
                decay = jnp.exp(jnp.where(causal, seg, NEG_BIG))
                ms.append((cb * decay * dt_t[hd:hd + 1, :]).astype(BF16))
                bws.append((b_t * w_t[hd:hd + 1, :]).astype(BF16))
                eas.append(jnp.exp(col))
                decs.append(jnp.exp(col[L - 1:L, :]))
            y2 = _dot(jnp.concatenate(ms, axis=0), x_bf)
            y_intra = jnp.where(first, y2[:L], y2[L:])
            ea = jnp.where(first, eas[0], eas[1])
            ybuf_ref[:, lo:lo + LANES] = y_intra + y_inter[:, pr * LANES:(pr + 1) * LANES] * ea
            s2 = _dot(jnp.concatenate(bws, axis=0), x_bf)
            s_new = jnp.where(first_n, s2[:nstate], s2[nstate:])
            dec = jnp.where(first[:1], decs[0], decs[1])
            state_ref[:, lo:lo + LANES] = state_ref[:, lo:lo + LANES] * dec + s_new

    xs = xbc_ref[:, :inner]
    z = z_ref[...]
    y = (ybuf_ref[...] + dskip_ref[...] * xs) * (z * _sigmoid(z))
    for g in range(groups):
        yg = y[:, g * gw:(g + 1) * gw]
        y_ref[:, g * gw:(g + 1) * gw] = (_rms(yg) * ng_ref[:, g * gw:(g + 1) * gw]).astype(y_ref.dtype)


def _ssd(xbc, dt, z, a_log, d_skip, norm_g, *, batch, seq, inner, heads):
    t, conv_dim = xbc.shape
    L = min(SSM_CHUNK, seq)
    nc = seq // L
    hdim = inner // heads
    alog = jnp.pad(a_log.astype(F32), (0, LANES - heads)).reshape(1, LANES)
    dskip = jnp.repeat(d_skip.astype(F32), hdim).reshape(1, inner)
    row = lambda b, c: (b * nc + c, 0)
    return pl.pallas_call(
        functools.partial(_ssd_kernel, inner=inner, groups=SSM_GROUPS, nstate=SSM_STATE, hdim=hdim),
        out_shape=jax.ShapeDtypeStruct((t, inner), BF16),
        grid=(batch, nc),
        in_specs=[pl.BlockSpec((L, conv_dim), row), pl.BlockSpec((L, LANES), row),
                  pl.BlockSpec((L, inner), row),
                  pl.BlockSpec((1, LANES), lambda b, c: (0, 0)),
                  pl.BlockSpec((1, inner), lambda b, c: (0, 0)),
                  pl.BlockSpec((1, inner), lambda b, c: (0, 0))],
        out_specs=pl.BlockSpec((L, inner), row),
        scratch_shapes=[pltpu.VMEM((SSM_STATE, inner), F32), pltpu.VMEM((L, inner), F32)],
        compiler_params=_params("parallel", "arbitrary"),
        name="ssd_scan",
    )(xbc, dt, z, alog, dskip, norm_g.reshape(1, inner))


def kernel(x, c, positions, w_mod, b_mod, ln_g, ln_b, ffn_w_in, ffn_w_out, mla_w_in, mla_q_norm_g, mla_kv_norm_g, mla_w_q_up, mla_w_kv_up, mla_w_out, diff_w_in, diff_lambda_q, diff_lambda_k, diff_subln_g, diff_w_out, ssm_w_in, ssm_conv_w, ssm_conv_b, ssm_dt_bias, ssm_a_log, ssm_d_skip, ssm_norm_g, ssm_w_out):
    batch, seq, d = x.shape
    depth = w_mod.shape[0]
    alpha = (2.0 * depth) ** 0.25
    t = batch * seq
    mod_all = _adaln(c, w_mod, b_mod).reshape(depth, batch, N_MOD_ROWS, d)
    x2 = x.reshape(t, d)
    lng = ln_g.reshape(depth, 3, 1, d)
    lnb = ln_b.reshape(depth, 3, 1, d)
    ffn_in = ffn_w_in.astype(BF16)
    ffn_out = ffn_w_out.astype(BF16)

    for i in range(depth):
        mod = mod_all[i]
        kind, idx = i % 3, i // 3
        x2 = _ffn(x2, mod, 0, ffn_in[i, 0], ffn_out[i, 0], lng[i, 0], lnb[i, 0], seq=seq, alpha=alpha)
        if kind == 0:
            heads = mla_w_out.shape[1] // MLA_V_DIM
            q, kn, kr, v = _mla_prep(x2, mod, 1, positions, mla_w_in[idx], mla_q_norm_g[idx],
                                     mla_kv_norm_g[idx], mla_w_q_up[idx], mla_w_kv_up[idx], seq=seq)
            a = _mla_attn(q, kn, kr, v, batch=batch, seq=seq, heads=heads)
            w_o = mla_w_out[idx]
        elif kind == 1:
            heads = d // (2 * DIFF_HEAD_DIM)
            lam_init = 0.8 - 0.6 * math.exp(-0.3 * i)
            col_scale = jnp.concatenate([jnp.full((1, d), DIFF_HEAD_DIM ** -0.5 * LOG2E, F32),
                                         jnp.ones((1, 2 * d), F32)], axis=1)
            qkv = _proj(x2, mod, 1, diff_w_in[idx].astype(BF16), col_scale, seq=seq)
            a = _diff_attn(qkv, diff_lambda_q[idx], diff_lambda_k[idx], diff_subln_g[idx],
                           batch=batch, seq=seq, heads=heads, lam_init=lam_init)
            w_o = diff_w_out[idx]
        else:
            heads = ssm_a_log.shape[1]
            inner = ssm_norm_g.shape[1]
            conv_dim = ssm_conv_w.shape[2]
            z, xbc, dt = _ssm_prep(x2, mod, 1, ssm_w_in[idx], ssm_conv_w[idx], ssm_conv_b[idx],
                                   ssm_dt_bias[idx], seq=seq, inner=inner, conv_dim=conv_dim, heads=heads)
            a = _ssd(xbc, dt, z, ssm_a_log[idx], ssm_d_skip[idx], ssm_norm_g[idx],
                     batch=batch, seq=seq, inner=inner, heads=heads)
            w_o = ssm_w_out[idx]
        x2 = _outproj(a, x2, mod, 1, w_o.astype(BF16), lng[i, 1], lnb[i, 1], seq=seq, alpha=alpha)
        x2 = _ffn(x2, mod, 2, ffn_in[i, 1], ffn_out[i, 1], lng[i, 2], lnb[i, 2], seq=seq, alpha=alpha)
    return x2.reshape(batch, seq, d)
```

```python
import functools
import math

import jax
import jax.numpy as jnp
from jax import lax
from jax.experimental import pallas as pl
from jax.experimental.pallas import tpu as pltpu

F32 = jnp.float32
BF16 = jnp.bfloat16

LN_EPS = 1e-5
RMS_EPS = 1e-6
FFN_RES_WEIGHT = 0.5
ROPE_THETA = 10000.0
MLA_NOPE_DIM = 128
MLA_ROPE_DIM = 64
MLA_V_DIM = 128
DIFF_HEAD_DIM = 64
SSM_HEAD_DIM = 64
SSM_STATE = 128
SSM_GROUPS = 4
SSM_CHUNK = 128
N_MOD_ROWS = 9
LANES = 128
NEG_BIG = -1e30
LOG2E = 1.4426950408889634

VMEM_LIMIT = 56 * 1024 * 1024

ROW_TILE = 512
FFN_SPLITS = 2
SSM_PREP_SPLITS = 2
ATTN_TILE = 256
DIFF_HEADS_PER_STEP = 4


def _params(*sem):
    return pltpu.CompilerParams(dimension_semantics=sem, vmem_limit_bytes=VMEM_LIMIT)


def _resident(shape):
    nd = len(shape)
    return pl.BlockSpec(shape, lambda *_: (0,) * nd, pipeline_mode=pl.Buffered(1))


def _dot(a, b):
    return jnp.dot(a, b, preferred_element_type=F32)


def _dot_nt(a, b):
    return lax.dot_general(a, b, (((1,), (1,)), ((), ())), preferred_element_type=F32)


def _sigmoid(v):
    return 1.0 / (1.0 + jnp.exp(-v))


def _modulated(x, mod_ref, j):
    m = mod_ref[0]
    return x * (1.0 + m[3 * j + 1:3 * j + 2]) + m[3 * j:3 * j + 1]


def _residual_layernorm(x, y, mod_ref, j, g_ref, b_ref, alpha, weight):
    gate = mod_ref[0][3 * j + 2:3 * j + 3]
    z = alpha * x + (weight * gate) * y
    mu = jnp.mean(z, axis=-1, keepdims=True)
    zc = z - mu
    var = jnp.mean(zc * zc, axis=-1, keepdims=True)
    return zc * lax.rsqrt(var + LN_EPS) * g_ref[...] + b_ref[...]


def _rms(v, eps=RMS_EPS):
    return v * lax.rsqrt(jnp.mean(v * v, axis=-1, keepdims=True) + eps)


def _adaln_kernel(c_ref, w_ref, b_ref, o_ref):
    c = c_ref[...]
    cond = (c * _sigmoid(c)).astype(BF16)
    o_ref[0] = _dot(cond, w_ref[0].astype(BF16)) + b_ref[0]


def _adaln(c, w_mod, b_mod):
    depth, d, n = w_mod.shape
    bsz = c.shape[0]
    tn = n // 8 if (n // 8) % LANES == 0 else n
    return pl.pallas_call(
        _adaln_kernel,
        out_shape=jax.ShapeDtypeStruct((depth, bsz, n), F32),
        grid=(depth, n // tn),
        in_specs=[pl.BlockSpec((bsz, d), lambda l, j: (0, 0)),
                  pl.BlockSpec((1, d, tn), lambda l, j: (l, 0, j)),
                  pl.BlockSpec((1, 1, tn), lambda l, j: (l, 0, j))],
        out_specs=pl.BlockSpec((1, bsz, tn), lambda l, j: (l, 0, j)),
        compiler_params=_params("parallel", "parallel"),
        name="adaln_mod",
    )(c, w_mod, b_mod.reshape(depth, 1, n))


def _ffn_kernel(*refs, j, alpha, hidden, splits, mixer_out):
    if mixer_out:
        a_ref, wo_ref, gm_ref, bm_ref, x_ref, mod_ref, win_ref, wout_ref, g_ref, b_ref, o_ref = refs
    else:
        x_ref, mod_ref, win_ref, wout_ref, g_ref, b_ref, o_ref = refs
    rows = x_ref.shape[0] // splits
    for sp in range(splits):
        sl = slice(sp * rows, (sp + 1) * rows)
        x = x_ref[sl, :]
        if mixer_out:
            x = _residual_layernorm(x, _dot(a_ref[sl, :], wo_ref[...]), mod_ref, j - 1, gm_ref, bm_ref, alpha, 1.0)
        h = _modulated(x, mod_ref, j).astype(BF16)
        gate = _dot(h, win_ref[:, :hidden])
        up = _dot(h, win_ref[:, hidden:])
        act = (gate * _sigmoid(gate) * up).astype(BF16)
        y = _dot(act, wout_ref[...])
        o_ref[sl, :] = _residual_layernorm(x, y, mod_ref, j, g_ref, b_ref, alpha, FFN_RES_WEIGHT)


def _ffn(x2, mod, j, w_in, w_out, g, b, *, seq, alpha, mixer=None):
    t, d = x2.shape
    hidden = w_out.shape[0]
    tm = min(ROW_TILE, seq)
    per_batch = seq // tm
    row = lambda i: (i, 0)
    args, specs = [], []
    if mixer is not None:
        a, w_o, gm, bm = mixer
        k = a.shape[1]
        args += [a, w_o, gm, bm]
        specs += [pl.BlockSpec((tm, k), row), _resident((k, d)), _resident((1, d)), _resident((1, d))]
    args += [x2, mod, w_in, w_out, g, b]
    specs += [pl.BlockSpec((tm, d), row),
              pl.BlockSpec((1, N_MOD_ROWS, d), lambda i: (i // per_batch, 0, 0)),
              _resident((d, 2 * hidden)), _resident((hidden, d)), _resident((1, d)), _resident((1, d))]
    return pl.pallas_call(
        functools.partial(_ffn_kernel, j=j, alpha=alpha, hidden=hidden, splits=FFN_SPLITS,
                          mixer_out=mixer is not None),
        out_shape=jax.ShapeDtypeStruct((t, d), F32),
        grid=(t // tm,),
        in_specs=specs,
        out_specs=pl.BlockSpec((tm, d), row),
        compiler_params=_params("parallel"),
        name="mixer_out_ffn" if mixer is not None else "ffn_sublayer",
    )(*args)


def _mla_prep_kernel(x_ref, mod_ref, pos_ref, fpat_ref, sgn_ref, win_ref, gq_ref, gkv_ref,
                     wqn_ref, wqr_ref, wkv_ref, q_ref, kn_ref, kr_ref, v_ref,
                     *, j, q_rank, kv_rank, heads, q_scale):
    h = _modulated(x_ref[...], mod_ref, j).astype(BF16)
    lat = _dot(h, win_ref[...])
    cq = lat[:, :q_rank]
    ckv = lat[:, q_rank:q_rank + kv_rank]
    k_rope = lat[:, q_rank + kv_rank:q_rank + kv_rank + LANES]
    k_rope_sw = lat[:, q_rank + kv_rank + LANES:q_rank + kv_rank + 2 * LANES]
    qn = (_rms(cq) * gq_ref[...]).astype(BF16)
    kvn = (_rms(ckv) * gkv_ref[...]).astype(BF16)

    ang = pos_ref[...].astype(F32) * fpat_ref[...]
    cos = jnp.cos(ang)
    sin = jnp.sin(ang) * sgn_ref[...]
    kr_ref[...] = (k_rope * cos + k_rope_sw * sin).astype(BF16)

    q_nope = _dot(qn, wqn_ref[...]) * q_scale
    q_rope = _dot(qn, wqr_ref[...])
    cos_q = cos * q_scale
    sin_q = sin * q_scale
    hw = heads * LANES
    for hd in range(heads):
        lo = hd * LANES
        q_ref[:, 2 * lo:2 * lo + LANES] = q_nope[:, lo:lo + LANES].astype(BF16)
        q_ref[:, 2 * lo + LANES:2 * lo + 2 * LANES] = (
            q_rope[:, lo:lo + LANES] * cos_q + q_rope[:, hw + lo:hw + lo + LANES] * sin_q).astype(BF16)
    kv = _dot(kvn, wkv_ref[...])
    kn_ref[...] = kv[:, :hw].astype(BF16)
    v_ref[...] = kv[:, hw:].astype(BF16)


def _mla_prep(x2, mod, j, pos, w_in, gq, gkv, w_q_up, w_kv_up, *, seq):
    t, d = x2.shape
    q_rank, kv_rank = gq.shape[0], gkv.shape[0]
    qk_dim = MLA_NOPE_DIM + MLA_ROPE_DIM
    heads = w_q_up.shape[1] // qk_dim
    half = MLA_ROPE_DIM // 2
    pad = LANES - MLA_ROPE_DIM

    def swap(wr):
        return jnp.concatenate([wr[..., half:], wr[..., :half]], axis=-1)

    def padl(wr):
        return jnp.pad(wr, [(0, 0)] * (wr.ndim - 1) + [(0, pad)])

    lo = q_rank + kv_rank
    w_kr = w_in[:, lo:lo + MLA_ROPE_DIM]
    w_in_p = jnp.concatenate([w_in[:, :lo], padl(w_kr), padl(swap(w_kr))], axis=1).astype(BF16)
    wq = w_q_up.reshape(q_rank, heads, qk_dim)
    w_qn = wq[:, :, :MLA_NOPE_DIM].reshape(q_rank, heads * LANES).astype(BF16)
    wq_r = wq[:, :, MLA_NOPE_DIM:]
    w_qr = jnp.concatenate([padl(wq_r).reshape(q_rank, heads * LANES),
                            padl(swap(wq_r)).reshape(q_rank, heads * LANES)], axis=1).astype(BF16)
    wkv = w_kv_up.reshape(kv_rank, heads, MLA_NOPE_DIM + MLA_V_DIM)
    w_kv = jnp.concatenate([wkv[:, :, :MLA_NOPE_DIM].reshape(kv_rank, heads * LANES),
                            wkv[:, :, MLA_NOPE_DIM:].reshape(kv_rank, heads * LANES)], axis=1).astype(BF16)

    inv_freq = ROPE_THETA ** (-jnp.arange(0, MLA_ROPE_DIM, 2, dtype=F32) / MLA_ROPE_DIM)
    zeros = jnp.zeros((pad,), F32)
    fpat = jnp.concatenate([inv_freq, inv_freq, zeros]).reshape(1, LANES)
    sgn = jnp.concatenate([-jnp.ones((half,), F32), jnp.ones((half,), F32), zeros]).reshape(1, LANES)

    tm = min(ROW_TILE, seq)
    per_batch = seq // tm
    hw = heads * LANES
    row = lambda i: (i, 0)
    return pl.pallas_call(
        functools.partial(_mla_prep_kernel, j=j, q_rank=q_rank, kv_rank=kv_rank, heads=heads,
                          q_scale=qk_dim ** -0.5 * LOG2E),
        out_shape=(jax.ShapeDtypeStruct((t, 2 * hw), BF16), jax.ShapeDtypeStruct((t, hw), BF16),
                   jax.ShapeDtypeStruct((t, LANES), BF16), jax.ShapeDtypeStruct((t, hw), BF16)),
        grid=(t // tm,),
        in_specs=[pl.BlockSpec((tm, d), row),
                  pl.BlockSpec((1, N_MOD_ROWS, d), lambda i: (i // per_batch, 0, 0)),
                  pl.BlockSpec((tm, 1), row),
                  _resident((1, LANES)), _resident((1, LANES)),
                  _resident(w_in_p.shape), _resident((1, q_rank)), _resident((1, kv_rank)),
                  _resident(w_qn.shape), _resident(w_qr.shape), _resident(w_kv.shape)],
        out_specs=(pl.BlockSpec((tm, 2 * hw), row), pl.BlockSpec((tm, hw), row),
                   pl.BlockSpec((tm, LANES), row), pl.BlockSpec((tm, hw), row)),
        compiler_params=_params("parallel"),
        name="mla_prep",
    )(x2, mod, pos.reshape(t, 1), fpat, sgn, w_in_p, gq.reshape(1, -1), gkv.reshape(1, -1),
      w_qn, w_qr, w_kv)


def _causal_softmax_pv(queries, load_k, load_v, qi, s_ref, m_ref, acc_ref, *, tq, dv):
    n = len(queries)
    rows = queries[0].shape[0]
    groups = tq // LANES
    r = lax.broadcasted_iota(jnp.int32, (tq, tq), 0)
    col = lax.broadcasted_iota(jnp.int32, (tq, tq), 1)
    causal = jnp.concatenate([col <= r] * (rows // tq), axis=0)

    def lane_fold(x, op):
        out = x[:, :LANES]
        for g in range(1, groups):
            out = op(out, x[:, g * LANES:(g + 1) * LANES])
        return out

    def scores(c, jb, masked):
        s = _dot_nt(queries[c], load_k(c, jb))
        if masked:
            s = jnp.where(causal, s, NEG_BIG)
        s_ref[c, jb] = s
        return lane_fold(s, jnp.maximum)

    def sweep(body):
        def pair(t, carry):
            body(2 * t, 2)
            return carry

        lax.fori_loop(0, qi // 2, pair, 0)

        @pl.when(qi % 2 == 1)
        def _():
            body(qi - 1, 1)

    for c in range(n):
        m_ref[c] = scores(c, qi, True)

    def pass1(jb0, count):
        for c in range(n):
            mx = scores(c, jb0, False)
            for u in range(1, count):
                mx = jnp.maximum(mx, scores(c, jb0 + u, False))
            m_ref[c] = jnp.maximum(m_ref[c], mx)

    sweep(pass1)

    for c in range(n):
        m_ref[c] = jnp.broadcast_to(jnp.max(m_ref[c], axis=-1, keepdims=True), (rows, LANES))

    def probs_v(c, jb0, count):
        mb = m_ref[c]
        ps = []
        for u in range(count):
            s = s_ref[c, jb0 + u]
            ps += [jnp.exp2(s[:, g * LANES:(g + 1) * LANES] - mb) for g in range(groups)]
        p = jnp.concatenate(ps, axis=1).astype(BF16)
        v = load_v(c, jb0, count)
        return _dot(p, jnp.concatenate([v, jnp.ones((v.shape[0], LANES), BF16)], axis=1))

    for c in range(n):
        acc_ref[c] = probs_v(c, qi, 1)

    def pass2(jb0, count):
        for c in range(n):
            acc_ref[c] += probs_v(c, jb0, count)

    sweep(pass2)
    return [acc_ref[c][:, :dv] / acc_ref[c][:, dv:] for c in range(n)]


def _attn_scratch(chains, nkv, rows, tq, dv):
    return [pltpu.VMEM((chains, nkv, rows, tq), F32), pltpu.VMEM((chains, rows, LANES), F32),
            pltpu.VMEM((chains, rows, dv + LANES), F32)]


def _mla_attn_kernel(q_ref, kn_ref, kr_ref, v_ref, o_ref, s_ref, m_ref, acc_ref, *, tq, heads):
    qi = pl.program_id(1)

    def rows_of(jb, count=1):
        return pl.ds(pl.multiple_of(jb * tq, tq), count * tq)

    def load_k(c, jb):
        return jnp.concatenate([kn_ref[rows_of(jb), c * LANES:(c + 1) * LANES], kr_ref[rows_of(jb), :]], axis=1)

    def load_v(c, jb, count):
        return v_ref[rows_of(jb, count), c * MLA_V_DIM:(c + 1) * MLA_V_DIM]

    queries = [q_ref[:, 2 * c * LANES:2 * (c + 1) * LANES] for c in range(heads)]
    outs = _causal_softmax_pv(queries, load_k, load_v, qi, s_ref, m_ref, acc_ref, tq=tq, dv=MLA_V_DIM)
    for c in range(heads):
        o_ref[:, c * MLA_V_DIM:(c + 1) * MLA_V_DIM] = outs[c].astype(o_ref.dtype)


def _mla_attn(q, kn, kr, v, *, batch, seq, heads):
    t = q.shape[0]
    tq = min(ATTN_TILE, seq)
    nq = seq // tq
    hw = heads * LANES
    return pl.pallas_call(
        functools.partial(_mla_attn_kernel, tq=tq, heads=heads),
        out_shape=jax.ShapeDtypeStruct((t, heads * MLA_V_DIM), BF16),
        grid=(batch, nq),
        in_specs=[pl.BlockSpec((tq, 2 * hw), lambda b, i: (b * nq + i, 0)),
                  pl.BlockSpec((seq, hw), lambda b, i: (b, 0)),
                  pl.BlockSpec((seq, LANES), lambda b, i: (b, 0)),
                  pl.BlockSpec((seq, hw), lambda b, i: (b, 0))],
        out_specs=pl.BlockSpec((tq, heads * MLA_V_DIM), lambda b, i: (b * nq + i, 0)),
        scratch_shapes=_attn_scratch(heads, nq, tq, tq, MLA_V_DIM),
        compiler_params=_params("parallel", "arbitrary"),
        name="mla_attn",
    )(q, kn, kr, v)


def _proj_kernel(x_ref, mod_ref, w_ref, cs_ref, o_ref, *, j):
    h = _modulated(x_ref[...], mod_ref, j).astype(BF16)
    o_ref[...] = (_dot(h, w_ref[...]) * cs_ref[...]).astype(o_ref.dtype)


def _proj(x2, mod, j, w, col_scale, *, seq):
    t, d = x2.shape
    n = w.shape[1]
    tm = min(ROW_TILE, seq)
    per_batch = seq // tm
    return pl.pallas_call(
        functools.partial(_proj_kernel, j=j),
        out_shape=jax.ShapeDtypeStruct((t, n), BF16),
        grid=(t // tm,),
        in_specs=[pl.BlockSpec((tm, d), lambda i: (i, 0)),
                  pl.BlockSpec((1, N_MOD_ROWS, d), lambda i: (i // per_batch, 0, 0)),
                  _resident((d, n)), _resident((1, n))],
        out_specs=pl.BlockSpec((tm, n), lambda i: (i, 0)),
        compiler_params=_params("parallel"),
        name="mod_proj",
    )(x2, mod, w, col_scale)


def _diff_attn_kernel(q_ref, k_ref, v_ref, lq_ref, lk_ref, g_ref, o_ref, s_ref, m_ref, acc_ref,
                      *, tq, heads, lam_init):
    qi = pl.program_id(2)

    def load_kv(ref):
        def load(c, jb, count=1):
            return ref[pl.ds(pl.multiple_of(jb * tq, tq), count * tq), c * LANES:(c + 1) * LANES]
        return load

    lane = lax.broadcasted_iota(jnp.int32, (tq, LANES), 1)
    queries = []
    for c in range(heads):
        q = q_ref[:, c * LANES:(c + 1) * LANES]
        zero = jnp.zeros_like(q)
        queries.append(jnp.concatenate([jnp.where(lane < DIFF_HEAD_DIM, q, zero),
                                        jnp.where(lane >= DIFF_HEAD_DIM, q, zero)], axis=0))
    outs = _causal_softmax_pv(queries, load_kv(k_ref), load_kv(v_ref), qi, s_ref, m_ref, acc_ref, tq=tq, dv=LANES)

    dots = jnp.sum(lq_ref[...] * lk_ref[...], axis=-1, keepdims=True)
    e = jnp.exp(dots)
    lam = e[0:1] - e[1:2] + lam_init
    for c in range(heads):
        o = outs[c][:tq] - lam * outs[c][tq:]
        o_ref[:, c * LANES:(c + 1) * LANES] = (_rms(o) * g_ref[...] * (1.0 - lam_init)).astype(o_ref.dtype)


def _diff_attn(qkv, lq, lk, g, *, batch, seq, heads, lam_init):
    t = qkv.shape[0]
    tq = min(ATTN_TILE, seq)
    nq = seq // tq
    hpg = min(DIFF_HEADS_PER_STEP, heads)
    ng = heads // hpg
    gw = hpg * LANES
    return pl.pallas_call(
        functools.partial(_diff_attn_kernel, tq=tq, heads=hpg, lam_init=lam_init),
        out_shape=jax.ShapeDtypeStruct((t, heads * LANES), BF16),
        grid=(batch, ng, nq),
        in_specs=[pl.BlockSpec((tq, gw), lambda b, h, i: (b * nq + i, h)),
                  pl.BlockSpec((seq, gw), lambda b, h, i: (b, ng + h)),
                  pl.BlockSpec((seq, gw), lambda b, h, i: (b, 2 * ng + h)),
                  pl.BlockSpec(lq.shape, lambda b, h, i: (0, 0)),
                  pl.BlockSpec(lk.shape, lambda b, h, i: (0, 0)),
                  pl.BlockSpec((1, LANES), lambda b, h, i: (0, 0))],
        out_specs=pl.BlockSpec((tq, gw), lambda b, h, i: (b * nq + i, h)),
        scratch_shapes=_attn_scratch(hpg, nq, 2 * tq, tq, LANES),
        compiler_params=_params("parallel", "parallel", "arbitrary"),
        name="diff_attn",
    )(qkv, qkv, qkv, lq, lk, g.reshape(1, LANES))


def _ssm_prep_kernel(x_ref, mod_ref, wz_ref, wx_ref, wdt_ref, cw_ref, cb_ref, dtb_ref,
                     z_ref, xbc_ref, dt_ref, buf_ref, *, j, per_batch, taps, splits):
    i = pl.program_id(0)
    tm = x_ref.shape[0]
    pad = 8

    @pl.when(i % per_batch == 0)
    def _():
        buf_ref[0:pad, :] = jnp.zeros((pad, buf_ref.shape[1]), F32)

    cw = cw_ref[...]
    bias = cb_ref[...]
    rows = tm // splits
    for sp in range(splits):
        lo = sp * rows
        h = _modulated(x_ref[lo:lo + rows, :], mod_ref, j).astype(BF16)
        z_ref[lo:lo + rows, :] = _dot(h, wz_ref[...])
        dt_raw = _dot(h, wdt_ref[...]) + dtb_ref[...]
        dt_ref[lo:lo + rows, :] = jnp.maximum(dt_raw, 0.0) + jnp.log1p(jnp.exp(-jnp.abs(dt_raw)))
        buf_ref[pad + lo:pad + lo + rows, :] = _dot(h, wx_ref[...])
        acc = bias
        for sh in range(taps):
            acc = acc + buf_ref[pad + lo - sh:pad + lo - sh + rows, :] * cw[taps - 1 - sh:taps - sh]
        xbc_ref[lo:lo + rows, :] = acc * _sigmoid(acc)
    buf_ref[0:pad, :] = buf_ref[tm:tm + pad, :]


def _ssm_prep(x2, mod, j, w_in, conv_w, conv_b, dt_bias, *, seq, inner, conv_dim, heads):
    t, d = x2.shape
    taps = conv_w.shape[0]
    w_z = w_in[:, :inner].astype(BF16)
    w_x = w_in[:, inner:inner + conv_dim].astype(BF16)
    w_dt = jnp.pad(w_in[:, inner + conv_dim:], ((0, 0), (0, LANES - heads))).astype(BF16)
    dtb = jnp.pad(dt_bias.astype(F32), (0, LANES - heads)).reshape(1, LANES)
    tm = min(ROW_TILE, seq)
    per_batch = seq // tm
    row = lambda i: (i, 0)
    return pl.pallas_call(
        functools.partial(_ssm_prep_kernel, j=j, per_batch=per_batch, taps=taps, splits=SSM_PREP_SPLITS),
        out_shape=(jax.ShapeDtypeStruct((t, inner), F32), jax.ShapeDtypeStruct((t, conv_dim), F32),
                   jax.ShapeDtypeStruct((t, LANES), F32)),
        grid=(t // tm,),
        in_specs=[pl.BlockSpec((tm, d), row),
                  pl.BlockSpec((1, N_MOD_ROWS, d), lambda i: (i // per_batch, 0, 0)),
                  _resident(w_z.shape), _resident(w_x.shape), _resident(w_dt.shape),
                  _resident((taps, conv_dim)), _resident((1, conv_dim)), _resident((1, LANES))],
        out_specs=(pl.BlockSpec((tm, inner), row), pl.BlockSpec((tm, conv_dim), row),
                   pl.BlockSpec((tm, LANES), row)),
        scratch_shapes=[pltpu.VMEM((tm + 8, conv_dim), F32)],
        compiler_params=_params("arbitrary"),
        name="ssm_prep",
    )(x2, mod, w_z, w_x, w_dt, conv_w, conv_b.reshape(1, conv_dim), dtb)


def _ssd_kernel(xbc_ref, dt_ref, z_ref, alog_ref, dskip_ref, ng_ref, y_ref, state_ref, ybuf_ref,
                *, inner, groups, nstate, hdim):
    c = pl.program_id(1)

    @pl.when(c == 0)
    def _():
        state_ref[...] = jnp.zeros_like(state_ref)

    L = dt_ref.shape[0]
    gw = inner // groups
    pairs_per_group = gw // LANES
    heads_per_group = gw // hdim

    dt = dt_ref[...]
    da = dt * (-jnp.exp(alog_ref[...]))
    ri = lax.broadcasted_iota(jnp.int32, (L, L), 0)
    ci = lax.broadcasted_iota(jnp.int32, (L, L), 1)
    causal = ci <= ri
    tril = causal.astype(F32)
    acum = jnp.dot(tril, da, preferred_element_type=F32, precision=lax.Precision.HIGHEST)
    acum_t = acum.T
    dt_t = dt.T
    w_t = jnp.exp(acum_t[:, L - 1:L] - acum_t) * dt_t
    lane = lax.broadcasted_iota(jnp.int32, (L, LANES), 1)
    first = lane < hdim
    first_n = lax.broadcasted_iota(jnp.int32, (nstate, LANES), 1) < hdim

    for g in range(groups):
        b_g = xbc_ref[:, inner + g * nstate:inner + (g + 1) * nstate]
        c_g = xbc_ref[:, inner + (groups + g) * nstate:inner + (groups + g + 1) * nstate]
        c_bf = c_g.astype(BF16)
        cb = _dot_nt(c_bf, b_g.astype(BF16))
        b_t = b_g.T
        y_inter = _dot(c_bf, state_ref[:, g * gw:(g + 1) * gw].astype(BF16))
        for pr in range(pairs_per_group):
            lo = g * gw + pr * LANES
            x_pair = xbc_ref[:, lo:lo + LANES]
            x_bf = x_pair.astype(BF16)
            ms, bws, eas, decs = [], [], [], []
            for sub in range(2):
                hd = g * heads_per_group + 2 * pr + sub
                col = jnp.broadcast_to(acum[:, hd:hd + 1], (L, LANES))
                seg = col - acum_t[hd:hd + 1, :]
                decay = jnp.exp(jnp.where(causal, seg, NEG_BIG))
                ms.append((cb * decay * dt_t[hd:hd + 1, :]).astype(BF16))
                bws.append((b_t * w_t[hd:hd + 1, :]).astype(BF16))
                eas.append(jnp.exp(col))
                decs.append(jnp.exp(col[L - 1:L, :]))
            y2 = _dot(jnp.concatenate(ms, axis=0), x_bf)
            y_intra = jnp.where(first, y2[:L], y2[L:])
            ea = jnp.where(first, eas[0], eas[1])
            ybuf_ref[:, lo:lo + LANES] = y_intra + y_inter[:, pr * LANES:(pr + 1) * LANES] * ea
            s2 = _dot(jnp.concatenate(bws, axis=0), x_bf)
            s_new = jnp.where(first_n, s2[:nstate], s2[nstate:])
            dec = jnp.where(first[:1], decs[0], decs[1])
            state_ref[:, lo:lo + LANES] = state_ref[:, lo:lo + LANES] * dec + s_new

    xs = xbc_ref[:, :inner]
    z = z_ref[...]
    y = (ybuf_ref[...] + dskip_ref[...] * xs) * (z * _sigmoid(z))
    for g in range(groups):
        yg = y[:, g * gw:(g + 1) * gw]
        y_ref[:, g * gw:(g + 1) * gw] = (_rms(yg) * ng_ref[:, g * gw:(g + 1) * gw]).astype(y_ref.dtype)


def _ssd(xbc, dt, z, a_log, d_skip, norm_g, *, batch, seq, inner, heads):
    t, conv_dim = xbc.shape
    L = min(SSM_CHUNK, seq)
    nc = seq // L
    hdim = inner // heads
    alog = jnp.pad(a_log.astype(F32), (0, LANES - heads)).reshape(1, LANES)
    dskip = jnp.repeat(d_skip.astype(F32), hdim).reshape(1, inner)
    row = lambda b, c: (b * nc + c, 0)
    return pl.pallas_call(
        functools.partial(_ssd_kernel, inner=inner, groups=SSM_GROUPS, nstate=SSM_STATE, hdim=hdim),
        out_shape=jax.ShapeDtypeStruct((t, inner), BF16),
        grid=(batch, nc),
        in_specs=[pl.BlockSpec((L, conv_dim), row), pl.BlockSpec((L, LANES), row),
                  pl.BlockSpec((L, inner), row),
                  pl.BlockSpec((1, LANES), lambda b, c: (0, 0)),
                  pl.BlockSpec((1, inner), lambda b, c: (0, 0)),
                  pl.BlockSpec((1, inner), lambda b, c: (0, 0))],
        out_specs=pl.BlockSpec((L, inner), row),
        scratch_shapes=[pltpu.VMEM((SSM_STATE, inner), F32), pltpu.VMEM((L, inner), F32)],
        compiler_params=_params("parallel", "arbitrary"),
        name="ssd_scan",
    )(xbc, dt, z, alog, dskip, norm_g.reshape(1, inner))


def kernel(x, c, positions, w_mod, b_mod, ln_g, ln_b, ffn_w_in, ffn_w_out, mla_w_in, mla_q_norm_g, mla_kv_norm_g, mla_w_q_up, mla_w_kv_up, mla_w_out, diff_w_in, diff_lambda_q, diff_lambda_k, diff_subln_g, diff_w_out, ssm_w_in, ssm_conv_w, ssm_conv_b, ssm_dt_bias, ssm_a_log, ssm_d_skip, ssm_norm_g, ssm_w_out):
    batch, seq, d = x.shape
    depth = w_mod.shape[0]
    alpha = (2.0 * depth) ** 0.25
    t = batch * seq
    mod_all = _adaln(c, w_mod, b_mod).reshape(depth, batch, N_MOD_ROWS, d)
    x2 = x.reshape(t, d)
    lng = ln_g.reshape(depth, 3, 1, d)
    lnb = ln_b.reshape(depth, 3, 1, d)
    ffn_in = ffn_w_in.astype(BF16)
    ffn_out = ffn_w_out.astype(BF16)

    for i in range(depth):
        mod = mod_all[i]
        kind, idx = i % 3, i // 3
        x2 = _ffn(x2, mod, 0, ffn_in[i, 0], ffn_out[i, 0], lng[i, 0], lnb[i, 0], seq=seq, alpha=alpha)
        if kind == 0:
            heads = mla_w_out.shape[1] // MLA_V_DIM
            q, kn, kr, v = _mla_prep(x2, mod, 1, positions, mla_w_in[idx], mla_q_norm_g[idx],
                                     mla_kv_norm_g[idx], mla_w_q_up[idx], mla_w_kv_up[idx], seq=seq)
            a = _mla_attn(q, kn, kr, v, batch=batch, seq=seq, heads=heads)
            w_o = mla_w_out[idx]
        elif kind == 1:
            heads = d // (2 * DIFF_HEAD_DIM)
            lam_init = 0.8 - 0.6 * math.exp(-0.3 * i)
            col_scale = jnp.concatenate([jnp.full((1, d), DIFF_HEAD_DIM ** -0.5 * LOG2E, F32),
                                         jnp.ones((1, 2 * d), F32)], axis=1)
            qkv = _proj(x2, mod, 1, diff_w_in[idx].astype(BF16), col_scale, seq=seq)
            a = _diff_attn(qkv, diff_lambda_q[idx], diff_lambda_k[idx], diff_subln_g[idx],
                           batch=batch, seq=seq, heads=heads, lam_init=lam_init)
            w_o = diff_w_out[idx]
        else:
            heads = ssm_a_log.shape[1]
            inner = ssm_norm_g.shape[1]
            conv_dim = ssm_conv_w.shape[2]
            z, xbc, dt = _ssm_prep(x2, mod, 1, ssm_w_in[idx], ssm_conv_w[idx], ssm_conv_b[idx],
                                   ssm_dt_bias[idx], seq=seq, inner=inner, conv_dim=conv_dim, heads=heads)
            a = _ssd(xbc, dt, z, ssm_a_log[idx], ssm_d_skip[idx], ssm_norm_g[idx],
                     batch=batch, seq=seq, inner=inner, heads=heads)
            w_o = ssm_w_out[idx]
        x2 = _ffn(x2, mod, 2, ffn_in[i, 1], ffn_out[i, 1], lng[i, 2], lnb[i, 2], seq=seq, alpha=alpha,
                  mixer=(a, w_o.astype(BF16), lng[i, 1], lnb[i, 1]))
    return x2.reshape(batch, seq, d)
```

```python
import functools
import math

import jax
import jax.numpy as jnp
from jax import lax
from jax.experimental import pallas as pl
from jax.experimental.pallas import tpu as pltpu

F32 = jnp.float32
BF16 = jnp.bfloat16

LN_EPS = 1e-5
RMS_EPS = 1e-6
FFN_RES_WEIGHT = 0.5
ROPE_THETA = 10000.0
MLA_NOPE_DIM = 128
MLA_ROPE_DIM = 64
MLA_V_DIM = 128
DIFF_HEAD_DIM = 64
SSM_HEAD_DIM = 64
SSM_STATE = 128
SSM_GROUPS = 4
SSM_CHUNK = 128
N_MOD_ROWS = 9
LANES = 128
NEG_BIG = -1e30
LOG2E = 1.4426950408889634

VMEM_LIMIT = 56 * 1024 * 1024

ROW_TILE = 512
FFN_TILE = 1024
FFN_SLAB_ROWS = 256
MLA_PREP_SPLITS = 1
SSM_PREP_SPLITS = 4
CONV_PHASES = 4
ATTN_TILE = 256
DIFF_HEADS_PER_STEP = 4


def _params(*sem):
    return pltpu.CompilerParams(dimension_semantics=sem, vmem_limit_bytes=VMEM_LIMIT)


def _resident(shape):
    nd = len(shape)
    return pl.BlockSpec(shape, lambda *_: (0,) * nd, pipeline_mode=pl.Buffered(1))


def _dot(a, b):
    return jnp.dot(a, b, preferred_element_type=F32)


def _dot_nt(a, b):
    return lax.dot_general(a, b, (((1,), (1,)), ((), ())), preferred_element_type=F32)


def _sigmoid(v):
    return 1.0 / (1.0 + jnp.exp(-v))


def _modulated(x, mod_ref, j):
    m = mod_ref[0]
    return x * (1.0 + m[3 * j + 1:3 * j + 2]) + m[3 * j:3 * j + 1]


def _residual_layernorm(x, y, mod_ref, j, g_ref, b_ref, alpha, weight):
    gate = mod_ref[0][3 * j + 2:3 * j + 3]
    z = alpha * x + (weight * gate) * y
    mu = jnp.mean(z, axis=-1, keepdims=True)
    zc = z - mu
    var = jnp.mean(zc * zc, axis=-1, keepdims=True)
    return zc * lax.rsqrt(var + LN_EPS) * g_ref[...] + b_ref[...]


def _rms(v, eps=RMS_EPS):
    return v * lax.rsqrt(jnp.mean(v * v, axis=-1, keepdims=True) + eps)


def _adaln_kernel(c_ref, w_ref, b_ref, o_ref):
    c = c_ref[...]
    cond = (c * _sigmoid(c)).astype(BF16)
    o_ref[0] = _dot(cond, w_ref[0].astype(BF16)) + b_ref[0]


def _adaln(c, w_mod, b_mod):
    depth, d, n = w_mod.shape
    bsz = c.shape[0]
    tn = n // 8 if (n // 8) % LANES == 0 else n
    return pl.pallas_call(
        _adaln_kernel,
        out_shape=jax.ShapeDtypeStruct((depth, bsz, n), F32),
        grid=(depth, n // tn),
        in_specs=[pl.BlockSpec((bsz, d), lambda l, j: (0, 0)),
                  pl.BlockSpec((1, d, tn), lambda l, j: (l, 0, j)),
                  pl.BlockSpec((1, 1, tn), lambda l, j: (l, 0, j))],
        out_specs=pl.BlockSpec((1, bsz, tn), lambda l, j: (l, 0, j)),
        compiler_params=_params("parallel", "parallel"),
        name="adaln_mod",
    )(c, w_mod, b_mod.reshape(depth, 1, n))


def _ffn_kernel(*refs, j, alpha, hidden, splits, mixer_out):
    if mixer_out:
        a_ref, wo_ref, gm_ref, bm_ref, x_ref, mod_ref, win_ref, wout_ref, g_ref, b_ref, o_ref = refs
    else:
        x_ref, mod_ref, win_ref, wout_ref, g_ref, b_ref, o_ref = refs
    rows = x_ref.shape[0] // splits
    slabs = [slice(sp * rows, (sp + 1) * rows) for sp in range(splits)]
    xs, hs = [], []
    for sl in slabs:
        x = x_ref[sl, :]
        if mixer_out:
            x = _residual_layernorm(x, _dot(a_ref[sl, :], wo_ref[...]), mod_ref, j - 1, gm_ref, bm_ref, alpha, 1.0)
        xs.append(x)
        hs.append(_modulated(x, mod_ref, j).astype(BF16))
    for sl, x, h in zip(slabs, xs, hs):
        gate = _dot(h, win_ref[:, :hidden])
        up = _dot(h, win_ref[:, hidden:])
        act = (gate * _sigmoid(gate) * up).astype(BF16)
        y = _dot(act, wout_ref[...])
        o_ref[sl, :] = _residual_layernorm(x, y, mod_ref, j, g_ref, b_ref, alpha, FFN_RES_WEIGHT)


def _ffn(x2, mod, j, w_in, w_out, g, b, *, seq, alpha, mixer=None):
    t, d = x2.shape
    hidden = w_out.shape[0]
    tm = min(FFN_TILE, seq)
    per_batch = seq // tm
    row = lambda i: (i, 0)
    args, specs = [], []
    if mixer is not None:
        a, w_o, gm, bm = mixer
        k = a.shape[1]
        args += [a, w_o, gm, bm]
        specs += [pl.BlockSpec((tm, k), row), _resident((k, d)), _resident((1, d)), _resident((1, d))]
    args += [x2, mod, w_in, w_out, g, b]
    specs += [pl.BlockSpec((tm, d), row),
              pl.BlockSpec((1, N_MOD_ROWS, d), lambda i: (i // per_batch, 0, 0)),
              _resident((d, 2 * hidden)), _resident((hidden, d)), _resident((1, d)), _resident((1, d))]
    return pl.pallas_call(
        functools.partial(_ffn_kernel, j=j, alpha=alpha, hidden=hidden,
                          splits=max(1, tm // FFN_SLAB_ROWS), mixer_out=mixer is not None),
        out_shape=jax.ShapeDtypeStruct((t, d), F32),
        grid=(t // tm,),
        in_specs=specs,
        out_specs=pl.BlockSpec((tm, d), row),
        compiler_params=_params("parallel"),
        name="mixer_out_ffn" if mixer is not None else "ffn_sublayer",
    )(*args)


def _mla_prep_kernel(x_ref, mod_ref, pos_ref, fpat_ref, sgn_ref, win_ref, gq_ref, gkv_ref,
                     wqn_ref, wqr_ref, wkv_ref, q_ref, kn_ref, kr_ref, v_ref,
                     *, j, q_rank, kv_rank, heads, q_scale, splits):
    hw = heads * LANES
    rows = x_ref.shape[0] // splits
    gq = gq_ref[...] * q_scale
    for sp in range(splits):
        sl = slice(sp * rows, (sp + 1) * rows)
        h = _modulated(x_ref[sl, :], mod_ref, j).astype(BF16)
        lat = _dot(h, win_ref[...])
        cq = lat[:, :q_rank]
        ckv = lat[:, q_rank:q_rank + kv_rank]
        k_rope = lat[:, q_rank + kv_rank:q_rank + kv_rank + LANES]
        k_rope_sw = lat[:, q_rank + kv_rank + LANES:q_rank + kv_rank + 2 * LANES]
        qn = (_rms(cq) * gq).astype(BF16)
        kvn = (_rms(ckv) * gkv_ref[...]).astype(BF16)

        ang = pos_ref[sl, :].astype(F32) * fpat_ref[...]
        cos = jnp.cos(ang)
        sin = jnp.sin(ang) * sgn_ref[...]
        kr_ref[sl, :] = (k_rope * cos + k_rope_sw * sin).astype(BF16)

        q_nope = _dot(qn, wqn_ref[...])
        q_rope = _dot(qn, wqr_ref[...])
        for hd in range(heads):
            lo = hd * LANES
            q_ref[sl, 2 * lo:2 * lo + LANES] = q_nope[:, lo:lo + LANES].astype(BF16)
            q_ref[sl, 2 * lo + LANES:2 * lo + 2 * LANES] = (
                q_rope[:, lo:lo + LANES] * cos + q_rope[:, hw + lo:hw + lo + LANES] * sin).astype(BF16)
        kv = _dot(kvn, wkv_ref[...])
        kn_ref[sl, :] = kv[:, :hw].astype(BF16)
        v_ref[sl, :] = kv[:, hw:].astype(BF16)


def _mla_prep(x2, mod, j, pos, w_in, gq, gkv, w_q_up, w_kv_up, *, seq):
    t, d = x2.shape
    q_rank, kv_rank = gq.shape[0], gkv.shape[0]
    qk_dim = MLA_NOPE_DIM + MLA_ROPE_DIM
    heads = w_q_up.shape[1] // qk_dim
    half = MLA_ROPE_DIM // 2
    pad = LANES - MLA_ROPE_DIM

    def swap(wr):
        return jnp.concatenate([wr[..., half:], wr[..., :half]], axis=-1)

    def padl(wr):
        return jnp.pad(wr, [(0, 0)] * (wr.ndim - 1) + [(0, pad)])

    lo = q_rank + kv_rank
    w_kr = w_in[:, lo:lo + MLA_ROPE_DIM]
    w_in_p = jnp.concatenate([w_in[:, :lo], padl(w_kr), padl(swap(w_kr))], axis=1).astype(BF16)
    wq = w_q_up.reshape(q_rank, heads, qk_dim)
    w_qn = wq[:, :, :MLA_NOPE_DIM].reshape(q_rank, heads * LANES).astype(BF16)
    wq_r = wq[:, :, MLA_NOPE_DIM:]
    w_qr = jnp.concatenate([padl(wq_r).reshape(q_rank, heads * LANES),
                            padl(swap(wq_r)).reshape(q_rank, heads * LANES)], axis=1).astype(BF16)
    wkv = w_kv_up.reshape(kv_rank, heads, MLA_NOPE_DIM + MLA_V_DIM)
    w_kv = jnp.concatenate([wkv[:, :, :MLA_NOPE_DIM].reshape(kv_rank, heads * LANES),
                            wkv[:, :, MLA_NOPE_DIM:].reshape(kv_rank, heads * LANES)], axis=1).astype(BF16)

    inv_freq = ROPE_THETA ** (-jnp.arange(0, MLA_ROPE_DIM, 2, dtype=F32) / MLA_ROPE_DIM)
    zeros = jnp.zeros((pad,), F32)
    fpat = jnp.concatenate([inv_freq, inv_freq, zeros]).reshape(1, LANES)
    sgn = jnp.concatenate([-jnp.ones((half,), F32), jnp.ones((half,), F32), zeros]).reshape(1, LANES)

    tm = min(ROW_TILE, seq)
    per_batch = seq // tm
    hw = heads * LANES
    row = lambda i: (i, 0)
    return pl.pallas_call(
        functools.partial(_mla_prep_kernel, j=j, q_rank=q_rank, kv_rank=kv_rank, heads=heads,
                          q_scale=qk_dim ** -0.5 * LOG2E, splits=MLA_PREP_SPLITS),
        out_shape=(jax.ShapeDtypeStruct((t, 2 * hw), BF16), jax.ShapeDtypeStruct((t, hw), BF16),
                   jax.ShapeDtypeStruct((t, LANES), BF16), jax.ShapeDtypeStruct((t, hw), BF16)),
        grid=(t // tm,),
        in_specs=[pl.BlockSpec((tm, d), row),
                  pl.BlockSpec((1, N_MOD_ROWS, d), lambda i: (i // per_batch, 0, 0)),
                  pl.BlockSpec((tm, 1), row),
                  _resident((1, LANES)), _resident((1, LANES)),
                  _resident(w_in_p.shape), _resident((1, q_rank)), _resident((1, kv_rank)),
                  _resident(w_qn.shape), _resident(w_qr.shape), _resident(w_kv.shape)],
        out_specs=(pl.BlockSpec((tm, 2 * hw), row), pl.BlockSpec((tm, hw), row),
                   pl.BlockSpec((tm, LANES), row), pl.BlockSpec((tm, hw), row)),
        compiler_params=_params("parallel"),
        name="mla_prep",
    )(x2, mod, pos.reshape(t, 1), fpat, sgn, w_in_p, gq.reshape(1, -1), gkv.reshape(1, -1),
      w_qn, w_qr, w_kv)


def _causal_softmax_pv(queries, load_k, load_v, qi, s_ref, m_ref, acc_ref, *, tq, dv):
    n = len(queries)
    rows = queries[0].shape[0]
    groups = tq // LANES
    r = lax.broadcasted_iota(jnp.int32, (tq, tq), 0)
    col = lax.broadcasted_iota(jnp.int32, (tq, tq), 1)
    causal = jnp.concatenate([col <= r] * (rows // tq), axis=0)

    def lane_fold(x, op):
        out = x[:, :LANES]
        for g in range(1, groups):
            out = op(out, x[:, g * LANES:(g + 1) * LANES])
        return out

    def scores(c, jb, masked):
        s = _dot_nt(queries[c], load_k(c, jb))
        if masked:
            s = jnp.where(causal, s, NEG_BIG)
        s_ref[c, jb] = s
        return lane_fold(s, jnp.maximum)

    def sweep(body):
        def pair(t, carry):
            body(2 * t, 2)
            return carry

        lax.fori_loop(0, qi // 2, pair, 0)

        @pl.when(qi % 2 == 1)
        def _():
            body(qi - 1, 1)

    for c in range(n):
        m_ref[c] = scores(c, qi, True)

    def pass1(jb0, count):
        for c in range(n):
            mx = scores(c, jb0, False)
            for u in range(1, count):
                mx = jnp.maximum(mx, scores(c, jb0 + u, False))
            m_ref[c] = jnp.maximum(m_ref[c], mx)

    sweep(pass1)

    for c in range(n):
        m_ref[c] = jnp.broadcast_to(jnp.max(m_ref[c], axis=-1, keepdims=True), (rows, LANES))

    def probs_v(c, jb0, count):
        mb = m_ref[c]
        ps = []
        for u in range(count):
            s = s_ref[c, jb0 + u]
            ps += [jnp.exp2(s[:, g * LANES:(g + 1) * LANES] - mb) for g in range(groups)]
        p = jnp.concatenate(ps, axis=1).astype(BF16)
        v = load_v(c, jb0, count)
        return _dot(p, jnp.concatenate([v, jnp.ones((v.shape[0], LANES), BF16)], axis=1))

    for c in range(n):
        acc_ref[c] = probs_v(c, qi, 1)

    def pass2(jb0, count):
        for c in range(n):
            acc_ref[c] += probs_v(c, jb0, count)

    sweep(pass2)
    return [acc_ref[c][:, :dv] / acc_ref[c][:, dv:] for c in range(n)]


def _attn_scratch(chains, nkv, rows, tq, dv):
    return [pltpu.VMEM((chains, nkv, rows, tq), F32), pltpu.VMEM((chains, rows, LANES), F32),
            pltpu.VMEM((chains, rows, dv + LANES), F32)]


def _mla_attn_kernel(q_ref, kn_ref, kr_ref, v_ref, o_ref, s_ref, m_ref, acc_ref, *, tq, heads):
    qi = pl.program_id(1)

    def rows_of(jb, count=1):
        return pl.ds(pl.multiple_of(jb * tq, tq), count * tq)

    def load_k(c, jb):
        return jnp.concatenate([kn_ref[rows_of(jb), c * LANES:(c + 1) * LANES], kr_ref[rows_of(jb), :]], axis=1)

    def load_v(c, jb, count):
        return v_ref[rows_of(jb, count), c * MLA_V_DIM:(c + 1) * MLA_V_DIM]

    queries = [q_ref[:, 2 * c * LANES:2 * (c + 1) * LANES] for c in range(heads)]
    outs = _causal_softmax_pv(queries, load_k, load_v, qi, s_ref, m_ref, acc_ref, tq=tq, dv=MLA_V_DIM)
    for c in range(heads):
        o_ref[:, c * MLA_V_DIM:(c + 1) * MLA_V_DIM] = outs[c].astype(o_ref.dtype)


def _mla_attn(q, kn, kr, v, *, batch, seq, heads):
    t = q.shape[0]
    tq = min(ATTN_TILE, seq)
    nq = seq // tq
    hw = heads * LANES
    return pl.pallas_call(
        functools.partial(_mla_attn_kernel, tq=tq, heads=heads),
        out_shape=jax.ShapeDtypeStruct((t, heads * MLA_V_DIM), BF16),
        grid=(batch, nq),
        in_specs=[pl.BlockSpec((tq, 2 * hw), lambda b, i: (b * nq + i, 0)),
                  pl.BlockSpec((seq, hw), lambda b, i: (b, 0)),
                  pl.BlockSpec((seq, LANES), lambda b, i: (b, 0)),
                  pl.BlockSpec((seq, hw), lambda b, i: (b, 0))],
        out_specs=pl.BlockSpec((tq, heads * MLA_V_DIM), lambda b, i: (b * nq + i, 0)),
        scratch_shapes=_attn_scratch(heads, nq, tq, tq, MLA_V_DIM),
        compiler_params=_params("parallel", "arbitrary"),
        name="mla_attn",
    )(q, kn, kr, v)


def _proj_kernel(x_ref, mod_ref, w_ref, cs_ref, o_ref, *, j):
    h = _modulated(x_ref[...], mod_ref, j).astype(BF16)
    o_ref[...] = (_dot(h, w_ref[...]) * cs_ref[...]).astype(o_ref.dtype)


def _proj(x2, mod, j, w, col_scale, *, seq):
    t, d = x2.shape
    n = w.shape[1]
    tm = min(ROW_TILE, seq)
    per_batch = seq // tm
    return pl.pallas_call(
        functools.partial(_proj_kernel, j=j),
        out_shape=jax.ShapeDtypeStruct((t, n), BF16),
        grid=(t // tm,),
        in_specs=[pl.BlockSpec((tm, d), lambda i: (i, 0)),
                  pl.BlockSpec((1, N_MOD_ROWS, d), lambda i: (i // per_batch, 0, 0)),
                  _resident((d, n)), _resident((1, n))],
        out_specs=pl.BlockSpec((tm, n), lambda i: (i, 0)),
        compiler_params=_params("parallel"),
        name="mod_proj",
    )(x2, mod, w, col_scale)


def _diff_attn_kernel(q_ref, k_ref, v_ref, lq_ref, lk_ref, g_ref, o_ref, s_ref, m_ref, acc_ref,
                      *, tq, heads, lam_init):
    qi = pl.program_id(2)

    def load_kv(ref):
        def load(c, jb, count=1):
            return ref[pl.ds(pl.multiple_of(jb * tq, tq), count * tq), c * LANES:(c + 1) * LANES]
        return load

    lane = lax.broadcasted_iota(jnp.int32, (tq, LANES), 1)
    queries = []
    for c in range(heads):
        q = q_ref[:, c * LANES:(c + 1) * LANES]
        zero = jnp.zeros_like(q)
        queries.append(jnp.concatenate([jnp.where(lane < DIFF_HEAD_DIM, q, zero),
                                        jnp.where(lane >= DIFF_HEAD_DIM, q, zero)], axis=0))
    outs = _causal_softmax_pv(queries, load_kv(k_ref), load_kv(v_ref), qi, s_ref, m_ref, acc_ref, tq=tq, dv=LANES)

    dots = jnp.sum(lq_ref[...] * lk_ref[...], axis=-1, keepdims=True)
    e = jnp.exp(dots)
    lam = e[0:1] - e[1:2] + lam_init
    for c in range(heads):
        o = outs[c][:tq] - lam * outs[c][tq:]
        o_ref[:, c * LANES:(c + 1) * LANES] = (_rms(o) * g_ref[...] * (1.0 - lam_init)).astype(o_ref.dtype)


def _diff_attn(qkv, lq, lk, g, *, batch, seq, heads, lam_init):
    t = qkv.shape[0]
    tq = min(ATTN_TILE, seq)
    nq = seq // tq
    hpg = min(DIFF_HEADS_PER_STEP, heads)
    ng = heads // hpg
    gw = hpg * LANES
    return pl.pallas_call(
        functools.partial(_diff_attn_kernel, tq=tq, heads=hpg, lam_init=lam_init),
        out_shape=jax.ShapeDtypeStruct((t, heads * LANES), BF16),
        grid=(batch, ng, nq),
        in_specs=[pl.BlockSpec((tq, gw), lambda b, h, i: (b * nq + i, h)),
                  pl.BlockSpec((seq, gw), lambda b, h, i: (b, ng + h)),
                  pl.BlockSpec((seq, gw), lambda b, h, i: (b, 2 * ng + h)),
                  pl.BlockSpec(lq.shape, lambda b, h, i: (0, 0)),
                  pl.BlockSpec(lk.shape, lambda b, h, i: (0, 0)),
                  pl.BlockSpec((1, LANES), lambda b, h, i: (0, 0))],
        out_specs=pl.BlockSpec((tq, gw), lambda b, h, i: (b * nq + i, h)),
        scratch_shapes=_attn_scratch(hpg, nq, 2 * tq, tq, LANES),
        compiler_params=_params("parallel", "parallel", "arbitrary"),
        name="diff_attn",
    )(qkv, qkv, qkv, lq, lk, g.reshape(1, LANES))


def _ssm_prep_kernel(x_ref, mod_ref, wz_ref, wx_ref, wdt_ref, cw_ref, cb_ref, dtb_ref,
                     z_ref, xbc_ref, dt_ref, buf_ref, *, j, per_batch, taps, splits):
    i = pl.program_id(0)
    tm = x_ref.shape[0]
    pad = 8
    nslab = buf_ref.shape[0]

    @pl.when(i % per_batch == 0)
    def _():
        buf_ref[:, 0:pad, :] = jnp.zeros((nslab, pad, LANES), F32)

    rows = tm // splits
    prows = rows // CONV_PHASES
    for sp in range(splits):
        lo = sp * rows
        h = _modulated(x_ref[lo:lo + rows, :], mod_ref, j).astype(BF16)
        z_ref[lo:lo + rows, :] = _dot(h, wz_ref[...])
        dt_raw = _dot(h, wdt_ref[...]) + dtb_ref[...]
        dt_ref[lo:lo + rows, :] = jnp.maximum(dt_raw, 0.0) + jnp.log1p(jnp.exp(-jnp.abs(dt_raw)))
        xbc = _dot(h, wx_ref[...])
        for s in range(nslab):
            buf_ref[s, pad + lo:pad + lo + rows, :] = xbc[:, s * LANES:(s + 1) * LANES]
        for s in range(nslab):
            w = cw_ref[:, s * LANES:(s + 1) * LANES]
            bias = cb_ref[:, s * LANES:(s + 1) * LANES]
            for ph in range(CONV_PHASES):
                acc = bias
                for sh in range(taps):
                    win = buf_ref[s, pl.ds(pad + lo + ph - sh, prows, stride=CONV_PHASES), :]
                    acc = acc + win * w[taps - 1 - sh:taps - sh]
                xbc_ref[s, pl.ds(lo + ph, prows, stride=CONV_PHASES), :] = acc * _sigmoid(acc)
    buf_ref[:, 0:pad, :] = buf_ref[:, tm:tm + pad, :]


def _ssm_prep(x2, mod, j, w_in, conv_w, conv_b, dt_bias, *, seq, inner, conv_dim, heads):
    t, d = x2.shape
    taps = conv_w.shape[0]
    w_z = w_in[:, :inner].astype(BF16)
    w_x = w_in[:, inner:inner + conv_dim].astype(BF16)
    w_dt = jnp.pad(w_in[:, inner + conv_dim:], ((0, 0), (0, LANES - heads))).astype(BF16)
    dtb = jnp.pad(dt_bias.astype(F32), (0, LANES - heads)).reshape(1, LANES)
    tm = min(ROW_TILE, seq)
    per_batch = seq // tm
    nslab = conv_dim // LANES
    row = lambda i: (i, 0)
    return pl.pallas_call(
        functools.partial(_ssm_prep_kernel, j=j, per_batch=per_batch, taps=taps, splits=SSM_PREP_SPLITS),
        out_shape=(jax.ShapeDtypeStruct((t, inner), F32), jax.ShapeDtypeStruct((nslab, t, LANES), F32),
                   jax.ShapeDtypeStruct((t, LANES), F32)),
        grid=(t // tm,),
        in_specs=[pl.BlockSpec((tm, d), row),
                  pl.BlockSpec((1, N_MOD_ROWS, d), lambda i: (i // per_batch, 0, 0)),
                  _resident(w_z.shape), _resident(w_x.shape), _resident(w_dt.shape),
                  _resident((taps, conv_dim)), _resident((1, conv_dim)), _resident((1, LANES))],
        out_specs=(pl.BlockSpec((tm, inner), row), pl.BlockSpec((nslab, tm, LANES), lambda i: (0, i, 0)),
                   pl.BlockSpec((tm, LANES), row)),
        scratch_shapes=[pltpu.VMEM((nslab, tm + 8, LANES), F32)],
        compiler_params=_params("arbitrary"),
        name="ssm_prep",
    )(x2, mod, w_z, w_x, w_dt, conv_w, conv_b.reshape(1, conv_dim), dtb)


def _ssd_kernel(xbc_ref, dt_ref, z_ref, alog_ref, dskip_ref, ng_ref, y_ref, state_ref, ybuf_ref,
                *, inner, groups, nstate, hdim):
    c = pl.program_id(1)

    @pl.when(c == 0)
    def _():
        state_ref[...] = jnp.zeros_like(state_ref)

    L = dt_ref.shape[0]
    gw = inner // groups
    pairs_per_group = gw // LANES
    heads_per_group = gw // hdim

    dt = dt_ref[...]
    da = dt * (-jnp.exp(alog_ref[...]))
    ri = lax.broadcasted_iota(jnp.int32, (L, L), 0)
    ci = lax.broadcasted_iota(jnp.int32, (L, L), 1)
    causal = ci <= ri
    tril = causal.astype(F32)
    acum = jnp.dot(tril, da, preferred_element_type=F32, precision=lax.Precision.HIGHEST)
    acum_t = acum.T
    dt_t = dt.T
    w_t = jnp.exp(acum_t[:, L - 1:L] - acum_t) * dt_t
    lane = lax.broadcasted_iota(jnp.int32, (L, LANES), 1)
    first = lane < hdim
    first_n = lax.broadcasted_iota(jnp.int32, (nstate, LANES), 1) < hdim

    for g in range(groups):
        b_g = xbc_ref[(inner + g * nstate) // LANES]
        c_g = xbc_ref[(inner + (groups + g) * nstate) // LANES]
        c_bf = c_g.astype(BF16)
        cb = _dot_nt(c_bf, b_g.astype(BF16))
        b_t = b_g.T
        y_inter = _dot(c_bf, state_ref[:, g * gw:(g + 1) * gw].astype(BF16))
        for pr in range(pairs_per_group):
            lo = g * gw + pr * LANES
            x_pair = xbc_ref[lo // LANES]
            x_bf = x_pair.astype(BF16)
            ms, bws, eas, decs = [], [], [], []
            for sub in range(2):
                hd = g * heads_per_group + 2 * pr + sub
                col = jnp.broadcast_to(acum[:, hd:hd + 1], (L, LANES))
                seg = col - acum_t[hd:hd + 1, :]
                decay = jnp.exp(jnp.where(causal, seg, NEG_BIG))
                ms.append((cb * decay * dt_t[hd:hd + 1, :]).astype(BF16))
                bws.append((b_t * w_t[hd:hd + 1, :]).astype(BF16))
                eas.append(jnp.exp(col))
                decs.append(jnp.exp(col[L - 1:L, :]))
            y2 = _dot(jnp.concatenate(ms, axis=0), x_bf)
            y_intra = jnp.where(first, y2[:L], y2[L:])
            ea = jnp.where(first, eas[0], eas[1])
            ybuf_ref[:, lo:lo + LANES] = y_intra + y_inter[:, pr * LANES:(pr + 1) * LANES] * ea
            s2 = _dot(jnp.concatenate(bws, axis=0), x_bf)
            s_new = jnp.where(first_n, s2[:nstate], s2[nstate:])
            dec = jnp.where(first[:1], decs[0], decs[1])
            state_ref[:, lo:lo + LANES] = state_ref[:, lo:lo + LANES] * dec + s_new

    xs = jnp.concatenate([xbc_ref[s] for s in range(inner // LANES)], axis=1)
    z = z_ref[...]
    y = (ybuf_ref[...] + dskip_ref[...] * xs) * (z * _sigmoid(z))
    for g in range(groups):
        yg = y[:, g * gw:(g + 1) * gw]
        y_ref[:, g * gw:(g + 1) * gw] = (_rms(yg) * ng_ref[:, g * gw:(g + 1) * gw]).astype(y_ref.dtype)


def _ssd(xbc, dt, z, a_log, d_skip, norm_g, *, batch, seq, inner, heads):
    nslab, t, _ = xbc.shape
    assert SSM_STATE == LANES
    L = min(SSM_CHUNK, seq)
    nc = seq // L
    hdim = inner // heads
    alog = jnp.pad(a_log.astype(F32), (0, LANES - heads)).reshape(1, LANES)
    dskip = jnp.repeat(d_skip.astype(F32), hdim).reshape(1, inner)
    row = lambda b, c: (b * nc + c, 0)
    return pl.pallas_call(
        functools.partial(_ssd_kernel, inner=inner, groups=SSM_GROUPS, nstate=SSM_STATE, hdim=hdim),
        out_shape=jax.ShapeDtypeStruct((t, inner), BF16),
        grid=(batch, nc),
        in_specs=[pl.BlockSpec((nslab, L, LANES), lambda b, c: (0, b * nc + c, 0)), pl.BlockSpec((L, LANES), row),
                  pl.BlockSpec((L, inner), row),
                  pl.BlockSpec((1, LANES), lambda b, c: (0, 0)),
                  pl.BlockSpec((1, inner), lambda b, c: (0, 0)),
                  pl.BlockSpec((1, inner), lambda b, c: (0, 0))],
        out_specs=pl.BlockSpec((L, inner), row),
        scratch_shapes=[pltpu.VMEM((SSM_STATE, inner), F32), pltpu.VMEM((L, inner), F32)],
        compiler_params=_params("parallel", "arbitrary"),
        name="ssd_scan",
    )(xbc, dt, z, alog, dskip, norm_g.reshape(1, inner))


def kernel(x, c, positions, w_mod, b_mod, ln_g, ln_b, ffn_w_in, ffn_w_out, mla_w_in, mla_q_norm_g, mla_kv_norm_g, mla_w_q_up, mla_w_kv_up, mla_w_out, diff_w_in, diff_lambda_q, diff_lambda_k, diff_subln_g, diff_w_out, ssm_w_in, ssm_conv_w, ssm_conv_b, ssm_dt_bias, ssm_a_log, ssm_d_skip, ssm_norm_g, ssm_w_out):
    batch, seq, d = x.shape
    depth = w_mod.shape[0]
    alpha = (2.0 * depth) ** 0.25
    t = batch * seq
    mod_all = _adaln(c, w_mod, b_mod).reshape(depth, batch, N_MOD_ROWS, d)
    x2 = x.reshape(t, d)
    lng = ln_g.reshape(depth, 3, 1, d)
    lnb = ln_b.reshape(depth, 3, 1, d)
    ffn_in = ffn_w_in.astype(BF16)
    ffn_out = ffn_w_out.astype(BF16)

    for i in range(depth):
        mod = mod_all[i]
        kind, idx = i % 3, i // 3
        x2 = _ffn(x2, mod, 0, ffn_in[i, 0], ffn_out[i, 0], lng[i, 0], lnb[i, 0], seq=seq, alpha=alpha)
        if kind == 0:
            heads = mla_w_out.shape[1] // MLA_V_DIM
            q, kn, kr, v = _mla_prep(x2, mod, 1, positions, mla_w_in[idx], mla_q_norm_g[idx],
                                     mla_kv_norm_g[idx], mla_w_q_up[idx], mla_w_kv_up[idx], seq=seq)
            a = _mla_attn(q, kn, kr, v, batch=batch, seq=seq, heads=heads)
            w_o = mla_w_out[idx]
        elif kind == 1:
            heads = d // (2 * DIFF_HEAD_DIM)
            lam_init = 0.8 - 0.6 * math.exp(-0.3 * i)
            col_scale = jnp.concatenate([jnp.full((1, d), DIFF_HEAD_DIM ** -0.5 * LOG2E, F32),
                                         jnp.ones((1, 2 * d), F32)], axis=1)
            qkv = _proj(x2, mod, 1, diff_w_in[idx].astype(BF16), col_scale, seq=seq)
            a = _diff_attn(qkv, diff_lambda_q[idx], diff_lambda_k[idx], diff_subln_g[idx],
                           batch=batch, seq=seq, heads=heads, lam_init=lam_init)
            w_o = diff_w_out[idx]
        else:
            heads = ssm_a_log.shape[1]
            inner = ssm_norm_g.shape[1]
            conv_dim = ssm_conv_w.shape[2]
            z, xbc, dt = _ssm_prep(x2, mod, 1, ssm_w_in[idx], ssm_conv_w[idx], ssm_conv_b[idx],
                                   ssm_dt_bias[idx], seq=seq, inner=inner, conv_dim=conv_dim, heads=heads)
            a = _ssd(xbc, dt, z, ssm_a_log[idx], ssm_d_skip[idx], ssm_norm_g[idx],
                     batch=batch, seq=seq, inner=inner, heads=heads)
            w_o = ssm_w_out[idx]
        x2 = _ffn(x2, mod, 2, ffn_in[i, 1], ffn_out[i, 1], lng[i, 2], lnb[i, 2], seq=seq, alpha=alpha,
                  mixer=(a, w_o.astype(BF16), lng[i, 1], lnb[i, 1]))
    return x2.reshape(batch, seq, d)
```
